```python
import math
import jax
import jax.numpy as jnp
from jax import lax
import numpy as np

D_MODEL = 1024
BATCH = 4
SEQ = 4096
DEPTH = 4
DEC_BATCH = 32
DEC_SEQ = 8
PAST_LEN = 8192
PAGE_SIZE = 128

N_EVEN = (DEPTH + 1) // 2
N_ODD = DEPTH // 2
NH_A = 4
DH_A = D_MODEL // 2 // NH_A
W_A = NH_A * DH_A
MLSTM_CHUNK = 128
NEG_STATE = -1e30
NH_B = 8
DH_B = D_MODEL // 2 // NH_B
W_B = NH_B * DH_B
MOBA_BLOCK = 256
MOBA_TOPK = 3
MOBA_Q_BLOCK = 16
NH_C = 8
DH_C = D_MODEL // (2 * NH_C)
ATTN_Q_BLOCK = 128
D_FF = ((8 * D_MODEL + 3 * 256 - 1) // (3 * 256)) * 256
E_IN_EVEN = 4 * W_A + 2 * NH_A + 3 * W_B
SPLIT_EVEN = (W_A, 2 * W_A, 3 * W_A, 4 * W_A, 4 * W_A + 2 * NH_A,
              4 * W_A + 2 * NH_A + W_B, 4 * W_A + 2 * NH_A + 2 * W_B)
EPS = 1e-6

kernel_name = 'mlstm_moba_diffattn_hybrid_step'


def rms_norm(x, g):
    xf = x.astype(jnp.float32)
    y = xf * lax.rsqrt(jnp.mean(xf * xf, axis=-1, keepdims=True) + EPS)
    return (y * g.astype(jnp.float32)).astype(x.dtype)


def gather_pages(cache, page_table):
    g = cache[page_table]
    return g.reshape((g.shape[0], -1) + cache.shape[2:])


def swiglu(h, w_gate, w_up, w_down):
    return (jax.nn.silu(h @ w_gate) * (h @ w_up)) @ w_down


def mlstm_chunkwise(q, k, v, i_pre, logf, c0, n0, m0):
    B, T, H, d = q.shape
    L = math.gcd(T, MLSTM_CHUNK)
    nc = T // L

    def to_chunks(a):
        a = a.astype(jnp.float32).reshape((B, nc, L) + a.shape[2:])
        return a.transpose((1, 0, 3, 2) + tuple(range(4, a.ndim)))

    causal = jnp.tril(jnp.ones((L, L), dtype=bool))

    def step(carry, inp):
        c, n, m = carry
        qc, kc, vc, ic, fc = inp
        b = jnp.cumsum(fc, axis=-1)
        log_d = jnp.where(causal, b[..., :, None] - b[..., None, :] + ic[..., None, :], -jnp.inf)
        inter = b + m[..., None]
        m_t = jnp.maximum(inter, jnp.max(log_d, axis=-1))
        w_intra = jnp.exp(log_d - m_t[..., None])
        w_inter = jnp.exp(inter - m_t)
        s = jnp.einsum('bhtd,bhsd->bhts', qc, kc) * w_intra
        num = w_inter[..., None] * jnp.einsum('bhed,bhtd->bhte', c, qc) + jnp.einsum('bhts,bhse->bhte', s, vc)
        den = w_inter * jnp.einsum('bhd,bhtd->bht', n, qc) + jnp.sum(s, axis=-1)
        h = num / jnp.maximum(jnp.abs(den), jnp.exp(-m_t))[..., None]
        m_new = m_t[..., -1]
        w_c = jnp.exp(b[..., -1] + m - m_new)
        w_s = jnp.exp(b[..., -1:] - b + ic - m_new[..., None])
        c_new = w_c[..., None, None] * c + jnp.einsum('bhs,bhse,bhsd->bhed', w_s, vc, kc)
        n_new = w_c[..., None] * n + jnp.einsum('bhs,bhsd->bhd', w_s, kc)
        return (c_new, n_new, m_new), h

    carry0 = (c0.astype(jnp.float32), n0.astype(jnp.float32), m0.astype(jnp.float32))
    xs = (to_chunks(q), to_chunks(k), to_chunks(v), to_chunks(i_pre), to_chunks(logf))
    (c, n, m), h = lax.scan(step, carry0, xs)
    h = h.transpose(1, 0, 3, 2, 4).reshape(B, T, H, d)
    return h, c, n, m


def moba_attention(q, k_all, v_all, q_pos):
    B, T, H, d = q.shape
    Lk = k_all.shape[1]
    nb = -(-Lk // MOBA_BLOCK)
    pad = ((0, 0), (0, nb * MOBA_BLOCK - Lk), (0, 0), (0, 0))
    k_blk = jnp.pad(k_all, pad).reshape(B, nb, MOBA_BLOCK, H, d).transpose(0, 3, 1, 2, 4)
    v_blk = jnp.pad(v_all, pad).reshape(B, nb, MOBA_BLOCK, H, d).transpose(0, 3, 1, 2, 4)
    k_mean = jnp.mean(k_blk.astype(jnp.float32), axis=3)
    cur = q_pos // MOBA_BLOCK
    gate = jnp.einsum('bthd,bhnd->bhtn', q.astype(jnp.float32), k_mean)
    gate = jnp.where(jnp.arange(nb)[None, :] < cur[:, None], gate, -jnp.inf)
    topk = min(MOBA_TOPK, nb)
    _, sel = lax.top_k(gate, topk)
    blocks = jnp.concatenate(
        [sel, jnp.broadcast_to(cur[None, None, :, None], (B, H, T, 1)).astype(sel.dtype)], axis=-1)
    qb = math.gcd(T, MOBA_Q_BLOCK)
    nc = T // qb
    q_c = q.reshape(B, nc, qb, H, d).transpose(1, 0, 2, 3, 4)
    blk_c = blocks.reshape(B, H, nc, qb, topk + 1).transpose(2, 0, 1, 3, 4)
    pos_c = q_pos.reshape(nc, qb)
    b_ix = jnp.arange(B)[:, None, None, None]
    h_ix = jnp.arange(H)[None, :, None, None]
    slot = jnp.arange(topk + 1)
    offs = jnp.arange(MOBA_BLOCK)
    scale = d ** -0.5

    def one(args):
        qi, bi, pi = args
        kg = k_blk[b_ix, h_ix, bi]
        vg = v_blk[b_ix, h_ix, bi]
        s = jnp.einsum('bqhd,bhqjsd->bhqjs', qi, kg).astype(jnp.float32) * scale
        key_pos = bi[..., None] * MOBA_BLOCK + offs
        own_ok = key_pos <= pi[None, None, :, None, None]
        sel_ok = (slot[None, :] < (pi // MOBA_BLOCK)[:, None])[None, None, :, :, None]
        ok = jnp.where((slot == topk)[:, None], own_ok, sel_ok)
        s = jnp.where(ok, s, -jnp.inf).reshape(B, H, qb, -1)
        p = jax.nn.softmax(s, axis=-1).reshape(B, H, qb, topk + 1, MOBA_BLOCK)
        return jnp.einsum('bhqjs,bhqjsd->bqhd', p.astype(vg.dtype), vg)

    o = lax.map(one, (q_c, blk_c, pos_c))
    return o.transpose(1, 0, 2, 3, 4).reshape(B, T, H, d)


def diff_attention(q, k_all, v_all, q_pos, lam):
    B, T, H, _, d = q.shape
    Lk = k_all.shape[1]
    qb = math.gcd(T, ATTN_Q_BLOCK)
    nc = T // qb
    q_c = q.reshape(B, nc, qb, H, 2, d).transpose(1, 0, 2, 3, 4, 5)
    pos_c = q_pos.reshape(nc, qb)
    k_pos = jnp.arange(Lk)
    scale = d ** -0.5

    def one(args):
        qi, pi = args
        s = jnp.einsum('bqhcd,bkhcd->bhcqk', qi, k_all).astype(jnp.float32) * scale
        s = jnp.where(k_pos[None, :] <= pi[:, None], s, -jnp.inf)
        p = jax.nn.softmax(s, axis=-1)
        a = p[:, :, 0] - lam * p[:, :, 1]
        return jnp.einsum('bhqk,bkhe->bqhe', a.astype(v_all.dtype), v_all)

    o = lax.map(one, (q_c, pos_c))
    return o.transpose(1, 0, 2, 3, 4).reshape(B, T, H, 2 * d)


def even_mixer(h, past_k, past_v, c0, n0, m0, w_in, b_gate, g_out, g_q, g_k, w_out):
    B, T, _ = h.shape
    past = past_k.shape[1]
    qa, ka, va, oa, ga, q_b, k_b, v_b = jnp.split(h @ w_in, SPLIT_EVEN, axis=-1)
    ga = ga.astype(jnp.float32) + b_gate.astype(jnp.float32)
    i_pre, f_pre = ga[..., :NH_A], ga[..., NH_A:]
    ha, c1, n1, m1 = mlstm_chunkwise(
        qa.reshape(B, T, NH_A, DH_A), ka.reshape(B, T, NH_A, DH_A) * DH_A ** -0.5,
        va.reshape(B, T, NH_A, DH_A), i_pre, jax.nn.log_sigmoid(f_pre), c0, n0, m0)
    ha = rms_norm(ha, g_out).astype(h.dtype) * jax.nn.sigmoid(oa.reshape(B, T, NH_A, DH_A))
    q_b = rms_norm(q_b.reshape(B, T, NH_B, DH_B), g_q)
    k_b = rms_norm(k_b.reshape(B, T, NH_B, DH_B), g_k)
    v_b = v_b.reshape(B, T, NH_B, DH_B)
    k_all = jnp.concatenate([past_k.astype(k_b.dtype), k_b], axis=1)
    v_all = jnp.concatenate([past_v.astype(v_b.dtype), v_b], axis=1)
    hb = moba_attention(q_b, k_all, v_all, past + jnp.arange(T))
    out = jnp.concatenate([ha.reshape(B, T, W_A), hb.reshape(B, T, W_B)], axis=-1) @ w_out
    return out, k_b, v_b, c1, n1, m1


def odd_mixer(h, past_k, past_v, w_in, g_q, g_k, lam_p, g_sub, w_out, lambda_init):
    B, T, _ = h.shape
    past = past_k.shape[1]
    q, k, v = jnp.split(h @ w_in, 3, axis=-1)
    q = rms_norm(q.reshape(B, T, NH_C, 2, DH_C), g_q)
    k = rms_norm(k.reshape(B, T, NH_C, 2, DH_C), g_k)
    v = v.reshape(B, T, NH_C, 2 * DH_C)
    lp = lam_p.astype(jnp.float32)
    lam = jnp.exp(jnp.sum(lp[0] * lp[1])) - jnp.exp(jnp.sum(lp[2] * lp[3])) + lambda_init
    k_rows = k.reshape(B, T, NH_C, 2 * DH_C)
    k_all = jnp.concatenate([past_k.astype(k.dtype), k_rows], axis=1).reshape(B, past + T, NH_C, 2, DH_C)
    v_all = jnp.concatenate([past_v.astype(v.dtype), v], axis=1)
    o = diff_attention(q, k_all, v_all, past + jnp.arange(T), lam)
    o = rms_norm(o, g_sub) * (1.0 - lambda_init)
    return o.reshape(B, T, D_MODEL) @ w_out, k_rows, v


def setup_inputs(seed: int = 0) -> dict:
    key = jax.random.key(seed)
    ks = jax.random.split(key, 32)
    f32 = jnp.float32

    def nrm(k, shape, s=1.0):
        return s * jax.random.normal(k, shape, f32)

    n_pages = PAST_LEN // PAGE_SIZE
    in_use = DEC_BATCH * n_pages
    n_pool = in_use + max(1, in_use // 4)
    page_table = jax.random.permutation(ks[0], n_pool)[:in_use].reshape(DEC_BATCH, n_pages).astype(jnp.int32)
    forget_bias = jnp.linspace(3.0, 6.0, NH_A, dtype=f32)
    b_gate = jnp.concatenate([nrm(ks[1], (N_EVEN, NH_A), 0.1),
                              forget_bias + nrm(ks[2], (N_EVEN, NH_A), 0.1)], axis=-1)
    return {
        'x_prompt': nrm(ks[3], (BATCH, SEQ, D_MODEL)),
        'x_sample': nrm(ks[4], (DEC_BATCH, DEC_SEQ, D_MODEL)),
        'cache_moba_k': nrm(ks[5], (N_EVEN, n_pool, PAGE_SIZE, NH_B, DH_B)),
        'cache_moba_v': nrm(ks[6], (N_EVEN, n_pool, PAGE_SIZE, NH_B, DH_B)),
        'cache_diff_k': nrm(ks[7], (N_ODD, n_pool, PAGE_SIZE, NH_C, 2 * DH_C)),
        'cache_diff_v': nrm(ks[8], (N_ODD, n_pool, PAGE_SIZE, NH_C, 2 * DH_C)),
        'state_mlstm_c': nrm(ks[9], (N_EVEN, DEC_BATCH, NH_A, DH_A, DH_A), 0.1),
        'state_mlstm_n': nrm(ks[10], (N_EVEN, DEC_BATCH, NH_A, DH_A), 0.1),
        'state_mlstm_m': nrm(ks[11], (N_EVEN, DEC_BATCH, NH_A)),
        'page_table': page_table,
        'norm_mix': 1.0 + nrm(ks[12], (DEPTH, D_MODEL), 0.05),
        'norm_ffn': 1.0 + nrm(ks[13], (DEPTH, D_MODEL), 0.05),
        'w_in_even': nrm(ks[14], (N_EVEN, D_MODEL, E_IN_EVEN), D_MODEL ** -0.5),
        'b_gate_even': b_gate,
        'g_out_mlstm': 1.0 + nrm(ks[15], (N_EVEN, DH_A), 0.05),
        'g_q_moba': 1.0 + nrm(ks[16], (N_EVEN, DH_B), 0.05),
        'g_k_moba': 1.0 + nrm(ks[17], (N_EVEN, DH_B), 0.05),
        'w_out_even': nrm(ks[18], (N_EVEN, W_A + W_B, D_MODEL), (W_A + W_B) ** -0.5),
        'w_in_odd': nrm(ks[19], (N_ODD, D_MODEL, 3 * D_MODEL), D_MODEL ** -0.5),
        'g_q_diff': 1.0 + nrm(ks[20], (N_ODD, 2, DH_C), 0.05),
        'g_k_diff': 1.0 + nrm(ks[21], (N_ODD, 2, DH_C), 0.05),
        'lam_diff': nrm(ks[22], (N_ODD, 4, DH_C), 0.1),
        'g_sub_diff': 1.0 + nrm(ks[23], (N_ODD, 2 * DH_C), 0.05),
        'w_out_odd': nrm(ks[24], (N_ODD, D_MODEL, D_MODEL), D_MODEL ** -0.5),
        'w_gate': nrm(ks[25], (DEPTH, D_MODEL, D_FF), D_MODEL ** -0.5),
        'w_up': nrm(ks[26], (DEPTH, D_MODEL, D_FF), D_MODEL ** -0.5),
        'w_down': nrm(ks[27], (DEPTH, D_FF, D_MODEL), D_FF ** -0.5),
    }


def reference(x_prompt, x_sample, cache_moba_k, cache_moba_v, cache_diff_k, cache_diff_v,
              state_mlstm_c, state_mlstm_n, state_mlstm_m, page_table,
              norm_mix, norm_ffn, w_in_even, b_gate_even, g_out_mlstm, g_q_moba, g_k_moba, w_out_even,
              w_in_odd, g_q_diff, g_k_diff, lam_diff, g_sub_diff, w_out_odd, w_gate, w_up, w_down):
    f32 = jnp.float32
    bp = x_prompt.shape[0]
    c_init = jnp.zeros((bp, NH_A, DH_A, DH_A), f32)
    n_init = jnp.zeros((bp, NH_A, DH_A), f32)
    m_init = jnp.full((bp, NH_A), NEG_STATE, f32)
    empty_b = jnp.zeros((bp, 0, NH_B, DH_B), x_prompt.dtype)
    empty_c = jnp.zeros((bp, 0, NH_C, 2 * DH_C), x_prompt.dtype)

    p_mk, p_mv, p_dk, p_dv, p_c, p_n, p_m = [], [], [], [], [], [], []
    s_mk, s_mv, s_dk, s_dv, s_c, s_n, s_m = [], [], [], [], [], [], []
    xp, xs = x_prompt, x_sample
    for layer in range(DEPTH):
        hp = rms_norm(xp, norm_mix[layer])
        hs = rms_norm(xs, norm_mix[layer])
        if layer % 2 == 0:
            e = layer // 2
            wts = (w_in_even[e], b_gate_even[e], g_out_mlstm[e], g_q_moba[e], g_k_moba[e], w_out_even[e])
            op, kp, vp, cp, np_, mp = even_mixer(hp, empty_b, empty_b, c_init, n_init, m_init, *wts)
            os_, ks_, vs_, cs, ns, ms = even_mixer(
                hs, gather_pages(cache_moba_k[e], page_table), gather_pages(cache_moba_v[e], page_table),
                state_mlstm_c[e], state_mlstm_n[e], state_mlstm_m[e], *wts)
            p_mk.append(kp); p_mv.append(vp); p_c.append(cp); p_n.append(np_); p_m.append(mp)
            s_mk.append(ks_); s_mv.append(vs_); s_c.append(cs); s_n.append(ns); s_m.append(ms)
        else:
            o = layer // 2
            lambda_init = 0.8 - 0.6 * math.exp(-0.3 * layer)
            wts = (w_in_odd[o], g_q_diff[o], g_k_diff[o], lam_diff[o], g_sub_diff[o], w_out_odd[o], lambda_init)
            op, kp, vp = odd_mixer(hp, empty_c, empty_c, *wts)
            os_, ks_, vs_ = odd_mixer(
                hs, gather_pages(cache_diff_k[o], page_table), gather_pages(cache_diff_v[o], page_table), *wts)
            p_dk.append(kp); p_dv.append(vp)
            s_dk.append(ks_); s_dv.append(vs_)
        xp = xp + op
        xs = xs + os_
        xp = xp + swiglu(rms_norm(xp, norm_ffn[layer]), w_gate[layer], w_up[layer], w_down[layer])
        xs = xs + swiglu(rms_norm(xs, norm_ffn[layer]), w_gate[layer], w_up[layer], w_down[layer])

    return (xp, xs,
            jnp.stack(p_mk), jnp.stack(p_mv), jnp.stack(p_dk), jnp.stack(p_dv),
            jnp.stack(p_c), jnp.stack(p_n), jnp.stack(p_m),
            jnp.stack(s_mk), jnp.stack(s_mv), jnp.stack(s_dk), jnp.stack(s_dv),
            jnp.stack(s_c), jnp.stack(s_n), jnp.stack(s_m))
```

```python
import functools
import math

import jax
import jax.numpy as jnp
from jax import lax
from jax.experimental import pallas as pl
from jax.experimental.pallas import tpu as pltpu

F32 = jnp.float32
BF16 = jnp.bfloat16

D_MODEL = 1024
NH_A, DH_A = 4, 128
W_A = NH_A * DH_A
NH_B, DH_B = 8, 64
W_B = NH_B * DH_B
NH_C, DH_C = 8, 64
MLSTM_CHUNK = 128
MOBA_BLOCK = 256
MOBA_TOPK = 3
PAGE = 128
EPS = 1e-6
NEG = -1e30
LANES = 128
FF_CHUNK = 256
VMEM_LIMIT = 56 * 1024 * 1024


def _dot(a, b):
    return jnp.dot(a, b, preferred_element_type=F32)


def _dot_nt(a, b):
    return lax.dot_general(a, b, (((1,), (1,)), ((), ())), preferred_element_type=F32)


def _rms(x, g):
    ms = jnp.mean(x * x, axis=-1, keepdims=True)
    return x * lax.rsqrt(ms + EPS) * g


def _half_norm(y, g):
    outs = []
    for c in range(y.shape[1] // LANES):
        yc = y[:, c * LANES:(c + 1) * LANES]
        sq = yc * yc
        lo = lax.broadcasted_iota(jnp.int32, yc.shape, 1) < 64
        s_lo = jnp.sum(jnp.where(lo, sq, 0.0), axis=-1, keepdims=True)
        s_hi = jnp.sum(jnp.where(lo, 0.0, sq), axis=-1, keepdims=True)
        ms = jnp.where(lo, s_lo, s_hi) * (1.0 / 64)
        outs.append(yc * lax.rsqrt(ms + EPS))
    return jnp.concatenate(outs, axis=-1) * g


def _params(sem):
    return pltpu.CompilerParams(dimension_semantics=sem, vmem_limit_bytes=VMEM_LIMIT)


def _resident(arr):
    nd = arr.ndim
    return pl.BlockSpec(arr.shape, lambda i: (0,) * nd, pipeline_mode=pl.Buffered(1))


def _norm_proj_body(x_ref, gn_ref, *refs, kinds, block_mean):
    n = len(kinds)
    w_refs, aux_refs, out_refs = refs[:n], refs[n:2 * n], refs[2 * n:]
    h = _rms(x_ref[...], gn_ref[...]).astype(BF16)
    for idx, kind in enumerate(kinds):
        w_ref, aux_ref, out_ref = w_refs[idx], aux_refs[idx], out_refs[idx]
        width = w_ref.shape[1]
        step = min(width, 512)
        for c0 in range(0, width, step):
            y = _dot(h, w_ref[:, c0:c0 + step])
            if kind == "bias":
                y = y + aux_ref[:, c0:c0 + step]
            elif kind == "half":
                y = _half_norm(y, aux_ref[:, c0:c0 + step])
            out_ref[:, c0:c0 + step] = y
            if block_mean == idx:
                mean_ref = out_refs[n]
                for r in range(y.shape[0] // MOBA_BLOCK):
                    blk = y[r * MOBA_BLOCK:(r + 1) * MOBA_BLOCK, :]
                    mean_ref[r] = jnp.mean(blk, axis=0, keepdims=True)


def norm_proj(x, gn, weights, auxes, kinds, tm, block_mean=None):
    n_rows = x.shape[0]
    grid = (n_rows // tm,)
    in_specs = [pl.BlockSpec((tm, D_MODEL), lambda i: (i, 0)), _resident(gn)]
    in_specs += [_resident(w) for w in weights] + [_resident(a) for a in auxes]
    out_shape = [jax.ShapeDtypeStruct((n_rows, w.shape[1]), F32) for w in weights]
    out_specs = [pl.BlockSpec((tm, w.shape[1]), lambda i: (i, 0)) for w in weights]
    if block_mean is not None:
        width = weights[block_mean].shape[1]
        out_shape.append(jax.ShapeDtypeStruct((n_rows // MOBA_BLOCK, 1, width), F32))
        out_specs.append(pl.BlockSpec((tm // MOBA_BLOCK, 1, width), lambda i: (i, 0, 0)))
    return pl.pallas_call(
        functools.partial(_norm_proj_body, kinds=tuple(kinds), block_mean=block_mean),
        grid=grid, in_specs=in_specs, out_specs=out_specs, out_shape=out_shape,
        compiler_params=_params(("parallel",)), name="norm_proj",
    )(x, gn, *weights, *auxes)


def _mix_ffn_body(x_ref, *refs, n_mix):
    a_refs = refs[:n_mix]
    wo_ref, gn_ref, wg_ref, wu_ref, wd_ref, out_ref = refs[n_mix:]
    mixed = jnp.concatenate([a_ref[...] for a_ref in a_refs], axis=-1).astype(BF16)
    x1 = x_ref[...] + _dot(mixed, wo_ref[...])
    h = _rms(x1, gn_ref[...]).astype(BF16)
    acc = x1
    for c in range(wg_ref.shape[0]):
        g = _dot(h, wg_ref[c])
        u = _dot(h, wu_ref[c])
        act = (g * jax.nn.sigmoid(g) * u).astype(BF16)
        acc = acc + _dot(act, wd_ref[c])
    out_ref[...] = acc


def mix_ffn(x, mixes, wo, gn, wg, wu, wd, tm):
    n_rows = x.shape[0]
    in_specs = [pl.BlockSpec((tm, D_MODEL), lambda i: (i, 0))]
    in_specs += [pl.BlockSpec((tm, a.shape[1]), lambda i: (i, 0)) for a in mixes]
    in_specs += [_resident(w) for w in (wo, gn, wg, wu, wd)]
    return pl.pallas_call(
        functools.partial(_mix_ffn_body, n_mix=len(mixes)),
        grid=(n_rows // tm,), in_specs=in_specs,
        out_specs=pl.BlockSpec((tm, D_MODEL), lambda i: (i, 0)),
        out_shape=jax.ShapeDtypeStruct((n_rows, D_MODEL), F32),
        compiler_params=_params(("parallel",)), name="mix_ffn",
    )(x, *mixes, wo, gn, wg, wu, wd)


def _log_sigmoid(x):
    return jnp.minimum(x, 0.0) - jnp.log1p(jnp.exp(-jnp.abs(x)))


def _gate_prep_body(g_ref, col_ref, row_ref, *, valid_len):
    chunk = MLSTM_CHUNK
    for c in range(g_ref.shape[0] // chunk):
        g = g_ref[c * chunk:(c + 1) * chunk, :]
        lane = lax.broadcasted_iota(jnp.int32, g.shape, 1)
        row = lax.broadcasted_iota(jnp.int32, g.shape, 0)
        is_f = (lane >= NH_A) & (lane < 2 * NH_A)
        live = row < valid_len
        logf = jnp.where(is_f & live, _log_sigmoid(g), 0.0)
        k = 1
        while k < chunk:
            logf = logf + jnp.where(row >= k, pltpu.roll(logf, k, 0), 0.0)
            k *= 2
        out = jnp.where(is_f, logf, jnp.where(live, g, NEG))
        col_ref[c * chunk:(c + 1) * chunk, :] = out
        row_ref[:, c * chunk:(c + 1) * chunk] = jnp.transpose(out)[:8, :]


def gate_prep(gates, valid_len, tm):
    n_rows = gates.shape[0]
    return pl.pallas_call(
        functools.partial(_gate_prep_body, valid_len=valid_len),
        grid=(n_rows // tm,),
        in_specs=[pl.BlockSpec((tm, LANES), lambda i: (i, 0))],
        out_specs=[pl.BlockSpec((tm, LANES), lambda i: (i, 0)), pl.BlockSpec((8, tm), lambda i: (0, i))],
        out_shape=[jax.ShapeDtypeStruct((n_rows, LANES), F32), jax.ShapeDtypeStruct((8, n_rows), F32)],
        compiler_params=_params(("parallel",)), name="gate_prep",
    )(gates)


def _mlstm_body(q_ref, k_ref, v_ref, o_ref, gc_ref, gr_ref, c0_ref, n0_ref, m0_ref, gout_ref,
                h_ref, c_out, n_out, m_out, c_s, n_s, m_s):
    head = pl.program_id(1)
    chunk_id = pl.program_id(2)
    L = MLSTM_CHUNK

    @pl.when(chunk_id == 0)
    def _():
        c_s[...] = c0_ref[0]
        n_s[...] = n0_ref[0]
        m_s[...] = m0_ref[0]

    q = q_ref[...]
    k = k_ref[...] * (DH_A ** -0.5)
    v = v_ref[...]
    gc = gc_ref[...]
    gr = gr_ref[...]
    lane = lax.broadcasted_iota(jnp.int32, gc.shape, 1)
    sub = lax.broadcasted_iota(jnp.int32, gr.shape, 0)
    i_col = jnp.sum(jnp.where(lane == head, gc, 0.0), axis=-1, keepdims=True)
    b_col = jnp.sum(jnp.where(lane == head + NH_A, gc, 0.0), axis=-1, keepdims=True)
    i_row = jnp.sum(jnp.where(sub == head, gr, 0.0), axis=0, keepdims=True)
    b_row = jnp.sum(jnp.where(sub == head + NH_A, gr, 0.0), axis=0, keepdims=True)
    m_prev = m_s[:, 0:1]
    b_last = b_col[L - 1:L, :]
    c_prev = c_s[...]
    n_prev = n_s[...]

    tt = lax.broadcasted_iota(jnp.int32, (L, L), 0)
    ss = lax.broadcasted_iota(jnp.int32, (L, L), 1)
    log_d = jnp.where(ss <= tt, b_col - b_row + i_row, NEG)
    inter = b_col + m_prev
    m_t = jnp.maximum(inter, jnp.max(log_d, axis=-1, keepdims=True))
    w_intra = jnp.exp(log_d - m_t)
    w_inter = jnp.exp(inter - m_t)
    qb, kb, vb = q.astype(BF16), k.astype(BF16), v.astype(BF16)
    s = _dot_nt(qb, kb) * w_intra
    num = w_inter * _dot_nt(qb, c_prev.astype(BF16)) + _dot(s.astype(BF16), vb)
    den = w_inter * jnp.sum(q * n_prev, axis=-1, keepdims=True) + jnp.sum(s, axis=-1, keepdims=True)
    hh = num / jnp.maximum(jnp.abs(den), jnp.exp(-m_t))
    m_new = m_t[L - 1:L, :]
    w_c = jnp.exp(b_last + m_prev - m_new)
    w_s = jnp.exp(b_last - b_col + i_col - m_new)
    vw_t = jnp.transpose(v * w_s).astype(BF16)
    c_new = w_c * c_prev + _dot(vw_t, kb)
    n_new = w_c * n_prev + jnp.sum(w_s * k, axis=0, keepdims=True)
    c_s[...] = c_new
    n_s[...] = n_new
    m_s[...] = jnp.broadcast_to(m_new, m_s.shape)

    hn = hh * lax.rsqrt(jnp.mean(hh * hh, axis=-1, keepdims=True) + EPS) * gout_ref[...]
    h_ref[...] = hn * jax.nn.sigmoid(o_ref[...])
    c_out[0] = c_new
    n_out[0] = n_new
    m_out[0] = jnp.broadcast_to(m_new, m_s.shape)


def mlstm(a, g_col, g_row, c0, n0, m0, g_out, batch, n_chunks):
    L = MLSTM_CHUNK
    n_rows = a.shape[0]

    def col(off):
        return pl.BlockSpec((L, DH_A), lambda b, h, c: (b * n_chunks + c, off + h))

    def state(shape):
        return pl.BlockSpec((1,) + shape, lambda b, h, c: (b * NH_A + h, 0, 0))

    in_specs = [col(0), col(NH_A), col(2 * NH_A), col(3 * NH_A),
                pl.BlockSpec((L, LANES), lambda b, h, c: (b * n_chunks + c, 0)),
                pl.BlockSpec((8, L), lambda b, h, c: (0, b * n_chunks + c)),
                state((DH_A, DH_A)), state((1, DH_A)), state((1, LANES)),
                pl.BlockSpec((1, DH_A), lambda b, h, c: (0, 0))]
    out_specs = [pl.BlockSpec((L, DH_A), lambda b, h, c: (b * n_chunks + c, h)),
                 state((DH_A, DH_A)), state((1, DH_A)), state((1, LANES))]
    out_shape = [jax.ShapeDtypeStruct((n_rows, W_A), F32),
                 jax.ShapeDtypeStruct((batch * NH_A, DH_A, DH_A), F32),
                 jax.ShapeDtypeStruct((batch * NH_A, 1, DH_A), F32),
                 jax.ShapeDtypeStruct((batch * NH_A, 1, LANES), F32)]
    return pl.pallas_call(
        _mlstm_body, grid=(batch, NH_A, n_chunks), in_specs=in_specs, out_specs=out_specs,
        out_shape=out_shape,
        scratch_shapes=[pltpu.VMEM((DH_A, DH_A), F32), pltpu.VMEM((1, DH_A), F32), pltpu.VMEM((1, LANES), F32)],
        compiler_params=_params(("parallel", "parallel", "arbitrary")), name="mlstm",
    )(a, a, a, a, g_col, g_row, c0, n0, m0, g_out)


def _topk_bias(gate, valid):
    lane = lax.broadcasted_iota(jnp.int32, gate.shape, 1)
    g = jnp.where(valid, gate, NEG)
    chosen = jnp.zeros(gate.shape, jnp.bool_)
    for _ in range(MOBA_TOPK):
        top = jnp.max(g, axis=-1, keepdims=True)
        first = jnp.min(jnp.where(g == top, lane, LANES), axis=-1, keepdims=True)
        pick = lane == first
        chosen = chosen | pick
        g = jnp.where(pick, NEG, g)
    return jnp.where(chosen & valid, 0.0, NEG)


def _lambda(lam_ref, lambda_init):
    lp = lam_ref[...]
    s01 = jnp.sum(lp[0:1, :] * lp[1:2, :], axis=-1, keepdims=True)
    s23 = jnp.sum(lp[2:3, :] * lp[3:4, :], axis=-1, keepdims=True)
    return jnp.exp(s01) - jnp.exp(s23) + lambda_init


def _flash_body(*refs, mode, tq, lambda_init):
    if mode == "moba":
        q_ref, k_ref, v_ref, km_ref, o_ref, m_s, l_s, acc_s, sel_s = refs
    else:
        q_ref, k_ref, v_ref, lam_ref, gsub_ref, o_ref, m_s, l_s, acc_s = refs
    i = pl.program_id(2)
    lane = lax.broadcasted_iota(jnp.int32, (tq, LANES), 1)
    lo = lane < 64
    qs = q_ref[...] * (DH_B ** -0.5)
    q2 = jnp.concatenate([jnp.where(lo, qs, 0.0), jnp.where(lo, 0.0, qs)], axis=0).astype(BF16)

    start = pl.multiple_of(i * tq, tq)
    kd = k_ref[pl.ds(start, tq), :].astype(BF16)
    vd = v_ref[pl.ds(start, tq), :].astype(BF16)
    s = _dot_nt(q2, kd)
    r = lax.broadcasted_iota(jnp.int32, (2 * tq, tq), 0)
    c = lax.broadcasted_iota(jnp.int32, (2 * tq, tq), 1)
    s = jnp.where(c <= jnp.where(r >= tq, r - tq, r), s, NEG)
    m0 = jnp.max(s, axis=-1, keepdims=True)
    p = jnp.exp(s - m0)
    m_s[...] = m0
    l_s[...] = jnp.sum(p, axis=-1, keepdims=True)
    acc_s[...] = _dot(p.astype(BF16), vd)

    lane2 = lax.broadcasted_iota(jnp.int32, (2 * tq, LANES), 1)
    if mode == "moba":
        gate = _dot_nt(q2, km_ref[0].astype(BF16))
        sel_s[...] = _topk_bias(gate, lane2 < i)

    def body(j, carry):
        st = pl.multiple_of(j * tq, tq)
        kj = k_ref[pl.ds(st, tq), :].astype(BF16)
        vj = v_ref[pl.ds(st, tq), :].astype(BF16)
        sj = _dot_nt(q2, kj)
        if mode == "moba":
            sj = sj + jnp.sum(jnp.where(lane2 == j, sel_s[...], 0.0), axis=-1, keepdims=True)
        m_old = m_s[...]
        m_new = jnp.maximum(m_old, jnp.max(sj, axis=-1, keepdims=True))
        alpha = jnp.exp(m_old - m_new)
        pj = jnp.exp(sj - m_new)
        l_s[...] = alpha * l_s[...] + jnp.sum(pj, axis=-1, keepdims=True)
        acc_s[...] = alpha * acc_s[...] + _dot(pj.astype(BF16), vj)
        m_s[...] = m_new
        return carry

    lax.fori_loop(0, i, body, 0)

    o2 = acc_s[...] / l_s[...]
    if mode == "moba":
        o_ref[...] = jnp.where(lo, o2[:tq], o2[tq:])
    else:
        lam = _lambda(lam_ref, lambda_init)
        o = o2[:tq] - lam * o2[tq:]
        on = o * lax.rsqrt(jnp.mean(o * o, axis=-1, keepdims=True) + EPS) * gsub_ref[...]
        o_ref[...] = on * (1.0 - lambda_init)


def flash_prompt(q, k, v, extras, mode, batch, seq, lambda_init=0.0):
    tq = MOBA_BLOCK
    groups = q.shape[1] // LANES
    nq = seq // tq
    in_specs = [pl.BlockSpec((tq, LANES), lambda b, g, i: (b * nq + i, g)),
                pl.BlockSpec((seq, LANES), lambda b, g, i: (b, g)),
                pl.BlockSpec((seq, LANES), lambda b, g, i: (b, g))]
    scratch = [pltpu.VMEM((2 * tq, 1), F32), pltpu.VMEM((2 * tq, 1), F32), pltpu.VMEM((2 * tq, LANES), F32)]
    if mode == "moba":
        in_specs.append(pl.BlockSpec((1, LANES, LANES), lambda b, g, i: (b, 0, g)))
        scratch.append(pltpu.VMEM((2 * tq, LANES), F32))
    else:
        in_specs += [pl.BlockSpec((4, DH_C), lambda b, g, i: (0, 0)), pl.BlockSpec((1, LANES), lambda b, g, i: (0, 0))]
    return pl.pallas_call(
        functools.partial(_flash_body, mode=mode, tq=tq, lambda_init=lambda_init),
        grid=(batch, groups, nq), in_specs=in_specs,
        out_specs=pl.BlockSpec((tq, LANES), lambda b, g, i: (b * nq + i, g)),
        out_shape=jax.ShapeDtypeStruct(q.shape, F32), scratch_shapes=scratch,
        compiler_params=_params(("parallel", "parallel", "arbitrary")), name="flash_" + mode,
    )(q, k, v, *extras)


PAGES_PER_STEP = 4


def _page_specs(layer, heads, width):
    def spec(r):
        return pl.BlockSpec((1, 1, PAGE, heads, width),
                            lambda b, s, pt: (layer, pt[b, PAGES_PER_STEP * s + r], 0, 0, 0))
    return [spec(r) for r in range(PAGES_PER_STEP)]


def _moba_sample_body(pt_ref, q_ref, kn_ref, vn_ref, *refs, t_new):
    k_refs, v_refs = refs[:PAGES_PER_STEP], refs[PAGES_PER_STEP:2 * PAGES_PER_STEP]
    o_ref, part_s, m_s, l_s, g_s = refs[2 * PAGES_PER_STEP:]
    step = pl.program_id(1)
    n_steps = pl.num_programs(1)
    rows = NH_B * t_new
    pages_per_block = MOBA_BLOCK // PAGE
    blocks_per_step = PAGES_PER_STEP // pages_per_block
    lane = lax.broadcasted_iota(jnp.int32, (rows, LANES), 1)

    @pl.when(step == 0)
    def _():
        m_s[...] = jnp.full(m_s.shape, NEG, F32)
        l_s[...] = jnp.zeros(l_s.shape, F32)
        g_s[...] = jnp.full(g_s.shape, NEG, F32)

    qs = q_ref[...] * (DH_B ** -0.5)
    q_heads = [qs[:, h * DH_B:(h + 1) * DH_B].astype(BF16) for h in range(NH_B)]

    def block_partial(k_of_head, v_of_head, mask):
        s = jnp.concatenate([_dot_nt(q_heads[h], k_of_head(h)) for h in range(NH_B)], axis=0)
        gate = jnp.sum(s, axis=-1, keepdims=True)
        if mask is not None:
            s = jnp.where(mask, s, NEG)
        m = jnp.max(s, axis=-1, keepdims=True)
        p = jnp.exp(s - m)
        l = jnp.sum(p, axis=-1, keepdims=True)
        o = jnp.concatenate([_dot(p[h * t_new:(h + 1) * t_new].astype(BF16), v_of_head(h))
                             for h in range(NH_B)], axis=0)
        return m, l, gate, o

    for r in range(blocks_per_step):
        pages = range(r * pages_per_block, (r + 1) * pages_per_block)

        def k_of_head(h, pages=pages):
            return jnp.concatenate([k_refs[p][0, 0, :, h, :] for p in pages], axis=0).astype(BF16)

        def v_of_head(h, pages=pages):
            return jnp.concatenate([v_refs[p][0, 0, :, h, :] for p in pages], axis=0).astype(BF16)

        m, l, gate, o = block_partial(k_of_head, v_of_head, None)
        j = step * blocks_per_step + r
        part_s[j] = o
        here = lane == j
        m_s[...] = jnp.where(here, m, m_s[...])
        l_s[...] = jnp.where(here, l, l_s[...])
        g_s[...] = jnp.where(here, gate, g_s[...])

    @pl.when(step == n_steps - 1)
    def _():
        n_past = n_steps * blocks_per_step
        pad = jnp.zeros((LANES - t_new, W_B), F32)
        kn = jnp.concatenate([kn_ref[...], pad], axis=0)
        vn = jnp.concatenate([vn_ref[...], pad], axis=0)
        rr = lax.broadcasted_iota(jnp.int32, (rows, LANES), 0)
        causal = lane <= rr % t_new
        m_own, l_own, _, o_own = block_partial(
            lambda h: kn[:, h * DH_B:(h + 1) * DH_B].astype(BF16),
            lambda h: vn[:, h * DH_B:(h + 1) * DH_B].astype(BF16), causal)
        bias = _topk_bias(g_s[...], lane < n_past)
        m_all = jnp.maximum(jnp.max(m_s[...] + bias, axis=-1, keepdims=True), m_own)
        w = jnp.exp(m_s[...] + bias - m_all)
        w_own = jnp.exp(m_own - m_all)
        den = jnp.sum(w * l_s[...], axis=-1, keepdims=True) + w_own * l_own
        acc = w_own * o_own
        for j in range(n_past):
            acc = acc + w[:, j:j + 1] * part_s[j]
        out = acc / den
        o_ref[...] = jnp.concatenate([out[h * t_new:(h + 1) * t_new] for h in range(NH_B)], axis=-1)


def moba_sample(page_table, qn, kn, vn, cache_k, cache_v, layer):
    batch, n_pages = page_table.shape
    t_new = qn.shape[0] // batch
    n_steps = n_pages // PAGES_PER_STEP
    n_blocks = n_pages * PAGE // MOBA_BLOCK
    rows = NH_B * t_new
    new = pl.BlockSpec((t_new, W_B), lambda b, s, pt: (b, 0))
    grid_spec = pltpu.PrefetchScalarGridSpec(
        num_scalar_prefetch=1, grid=(batch, n_steps),
        in_specs=[new, new, new] + _page_specs(layer, NH_B, DH_B) + _page_specs(layer, NH_B, DH_B),
        out_specs=new,
        scratch_shapes=[pltpu.VMEM((n_blocks, rows, DH_B), F32), pltpu.VMEM((rows, LANES), F32),
                        pltpu.VMEM((rows, LANES), F32), pltpu.VMEM((rows, LANES), F32)])
    return pl.pallas_call(
        functools.partial(_moba_sample_body, t_new=t_new), grid_spec=grid_spec,
        out_shape=jax.ShapeDtypeStruct(qn.shape, F32),
        compiler_params=_params(("parallel", "arbitrary")), name="moba_sample",
    )(page_table, qn, kn, vn, *([cache_k] * PAGES_PER_STEP), *([cache_v] * PAGES_PER_STEP))


def _diff_sample_body(pt_ref, q_ref, kn_ref, vn_ref, lam_ref, gsub_ref, *refs, t_new, lambda_init):
    k_refs, v_refs = refs[:PAGES_PER_STEP], refs[PAGES_PER_STEP:2 * PAGES_PER_STEP]
    o_ref, m_s, l_s, acc_s = refs[2 * PAGES_PER_STEP:]
    step = pl.program_id(1)
    n_steps = pl.num_programs(1)
    per_head = 2 * t_new
    rows = NH_C * per_head

    @pl.when(step == 0)
    def _():
        m_s[...] = jnp.full(m_s.shape, NEG, F32)
        l_s[...] = jnp.zeros(l_s.shape, F32)
        acc_s[...] = jnp.zeros(acc_s.shape, F32)

    qs = q_ref[...] * (DH_C ** -0.5)
    lo = lax.broadcasted_iota(jnp.int32, (t_new, LANES), 1) < DH_C
    q_heads = []
    for h in range(NH_C):
        qh = qs[:, h * LANES:(h + 1) * LANES]
        q_heads.append(jnp.concatenate([jnp.where(lo, qh, 0.0), jnp.where(lo, 0.0, qh)], axis=0).astype(BF16))

    def update(k_of_head, v_of_head, mask):
        s = jnp.concatenate([_dot_nt(q_heads[h], k_of_head(h)) for h in range(NH_C)], axis=0)
        if mask is not None:
            s = jnp.where(mask, s, NEG)
        m_old = m_s[...]
        m_new = jnp.maximum(m_old, jnp.max(s, axis=-1, keepdims=True))
        alpha = jnp.exp(m_old - m_new)
        p = jnp.exp(s - m_new)
        l_s[...] = alpha * l_s[...] + jnp.sum(p, axis=-1, keepdims=True)
        pv = jnp.concatenate([_dot(p[h * per_head:(h + 1) * per_head].astype(BF16), v_of_head(h))
                              for h in range(NH_C)], axis=0)
        acc_s[...] = alpha * acc_s[...] + pv
        m_s[...] = m_new

    update(lambda h: jnp.concatenate([r[0, 0, :, h, :] for r in k_refs], axis=0).astype(BF16),
           lambda h: jnp.concatenate([r[0, 0, :, h, :] for r in v_refs], axis=0).astype(BF16), None)

    @pl.when(step == n_steps - 1)
    def _():
        pad = jnp.zeros((LANES - t_new, NH_C * LANES), F32)
        kn = jnp.concatenate([kn_ref[...], pad], axis=0)
        vn = jnp.concatenate([vn_ref[...], pad], axis=0)
        rr = lax.broadcasted_iota(jnp.int32, (rows, LANES), 0)
        cc = lax.broadcasted_iota(jnp.int32, (rows, LANES), 1)
        update(lambda h: kn[:, h * LANES:(h + 1) * LANES].astype(BF16),
               lambda h: vn[:, h * LANES:(h + 1) * LANES].astype(BF16), cc <= rr % t_new)
        o2 = acc_s[...] / l_s[...]
        lam = _lambda(lam_ref, lambda_init)
        outs = []
        for h in range(NH_C):
            base = h * per_head
            o = o2[base:base + t_new] - lam * o2[base + t_new:base + per_head]
            on = o * lax.rsqrt(jnp.mean(o * o, axis=-1, keepdims=True) + EPS) * gsub_ref[...]
            outs.append(on * (1.0 - lambda_init))
        o_ref[...] = jnp.concatenate(outs, axis=-1)


def diff_sample(page_table, qn, kn, vn, lam, gsub, cache_k, cache_v, layer, lambda_init):
    batch, n_pages = page_table.shape
    t_new = qn.shape[0] // batch
    n_steps = n_pages // PAGES_PER_STEP
    rows = NH_C * 2 * t_new
    new = pl.BlockSpec((t_new, NH_C * LANES), lambda b, s, pt: (b, 0))
    grid_spec = pltpu.PrefetchScalarGridSpec(
        num_scalar_prefetch=1, grid=(batch, n_steps),
        in_specs=[new, new, new,
                  pl.BlockSpec((4, DH_C), lambda b, s, pt: (0, 0)),
                  pl.BlockSpec((1, LANES), lambda b, s, pt: (0, 0))]
        + _page_specs(layer, NH_C, LANES) + _page_specs(layer, NH_C, LANES),
        out_specs=new,
        scratch_shapes=[pltpu.VMEM((rows, 1), F32), pltpu.VMEM((rows, 1), F32), pltpu.VMEM((rows, LANES), F32)])
    return pl.pallas_call(
        functools.partial(_diff_sample_body, t_new=t_new, lambda_init=lambda_init), grid_spec=grid_spec,
        out_shape=jax.ShapeDtypeStruct(qn.shape, F32),
        compiler_params=_params(("parallel", "arbitrary")), name="diff_sample",
    )(page_table, qn, kn, vn, lam, gsub, *([cache_k] * PAGES_PER_STEP), *([cache_v] * PAGES_PER_STEP))


def _row(v):
    return v.reshape(1, -1).astype(F32)


def _even_weights(w_in, b_gate, g_q, g_k):
    w_a = w_in[:, :4 * W_A].astype(BF16)
    g0 = 4 * W_A
    w_g = jnp.pad(w_in[:, g0:g0 + 2 * NH_A], ((0, 0), (0, LANES - 2 * NH_A))).astype(BF16)
    b0 = g0 + 2 * NH_A
    w_q, w_k, w_v = (w_in[:, b0 + i * W_B:b0 + (i + 1) * W_B].astype(BF16) for i in range(3))
    bias = jnp.pad(_row(b_gate), ((0, 0), (0, LANES - 2 * NH_A)))
    zeros = jnp.zeros((1, W_B), F32)
    weights = [w_a, w_g, w_q, w_k, w_v]
    auxes = [jnp.zeros((1, 4 * W_A), F32), bias, _row(jnp.tile(g_q, NH_B)), _row(jnp.tile(g_k, NH_B)), zeros]
    return weights, auxes, ["none", "bias", "half", "half", "none"]


def _ffn_weights(w_gate, w_up, w_down):
    d_ff = w_gate.shape[1]
    n = d_ff // FF_CHUNK
    wg = w_gate.astype(BF16).reshape(D_MODEL, n, FF_CHUNK).transpose(1, 0, 2)
    wu = w_up.astype(BF16).reshape(D_MODEL, n, FF_CHUNK).transpose(1, 0, 2)
    wd = w_down.astype(BF16).reshape(n, FF_CHUNK, D_MODEL)
    return wg, wu, wd


def _mlstm_group(a, gates, c0, n0, m0, g_out, batch, seq):
    L = MLSTM_CHUNK
    if seq % L:
        a = jnp.pad(a.reshape(batch, seq, -1), ((0, 0), (0, L - seq), (0, 0))).reshape(batch * L, -1)
        gates = jnp.pad(gates.reshape(batch, seq, -1), ((0, 0), (0, L - seq), (0, 0))).reshape(batch * L, -1)
        valid, n_chunks = seq, 1
    else:
        valid, n_chunks = L, seq // L
    g_col, g_row = gate_prep(gates, valid, min(512, gates.shape[0]))
    ha, c1, n1, m1 = mlstm(a, g_col, g_row, c0.reshape(batch * NH_A, DH_A, DH_A),
                           n0.reshape(batch * NH_A, 1, DH_A),
                           jnp.broadcast_to(m0.reshape(batch * NH_A, 1, 1), (batch * NH_A, 1, LANES)),
                           g_out, batch, n_chunks)
    if seq % L:
        ha = ha.reshape(batch, L, W_A)[:, :seq].reshape(batch * seq, W_A)
    return (ha, c1.reshape(batch, NH_A, DH_A, DH_A), n1.reshape(batch, NH_A, DH_A),
            m1[:, 0, 0].reshape(batch, NH_A))


def kernel(x_prompt, x_sample, cache_moba_k, cache_moba_v, cache_diff_k, cache_diff_v, state_mlstm_c, state_mlstm_n, state_mlstm_m, page_table, norm_mix, norm_ffn, w_in_even, b_gate_even, g_out_mlstm, g_q_moba, g_k_moba, w_out_even, w_in_odd, g_q_diff, g_k_diff, lam_diff, g_sub_diff, w_out_odd, w_gate, w_up, w_down):
    bp, seq, _ = x_prompt.shape
    bs, dec, _ = x_sample.shape
    n_p, n_s = bp * seq, bs * dec
    tm_p, tm_s = 512, n_s
    depth = norm_mix.shape[0]
    xp = x_prompt.reshape(n_p, D_MODEL)
    xs = x_sample.reshape(n_s, D_MODEL)
    c_init = jnp.zeros((bp, NH_A, DH_A, DH_A), F32)
    n_init = jnp.zeros((bp, NH_A, DH_A), F32)
    m_init = jnp.full((bp, NH_A), NEG, F32)
    n_kblocks = seq // MOBA_BLOCK

    outs = {k: [] for k in ("p_mk", "p_mv", "p_dk", "p_dv", "p_c", "p_n", "p_m",
                            "s_mk", "s_mv", "s_dk", "s_dv", "s_c", "s_n", "s_m")}
    for layer in range(depth):
        gn_mix = _row(norm_mix[layer])
        gn_ffn = _row(norm_ffn[layer])
        wg, wu, wd = _ffn_weights(w_gate[layer], w_up[layer], w_down[layer])
        if layer % 2 == 0:
            e = layer // 2
            weights, auxes, kinds = _even_weights(w_in_even[e], b_gate_even[e], g_q_moba[e], g_k_moba[e])
            g_out = _row(g_out_mlstm[e])
            wo = w_out_even[e].astype(BF16)

            a, gates, qb, kb, vb, kmean = norm_proj(xp, gn_mix, weights, auxes, kinds, tm_p, block_mean=3)
            ha, c1, n1, m1 = _mlstm_group(a, gates, c_init, n_init, m_init, g_out, bp, seq)
            kmean = jnp.pad(kmean.reshape(bp, n_kblocks, W_B), ((0, 0), (0, LANES - n_kblocks), (0, 0)))
            hb = flash_prompt(qb, kb, vb, [kmean], "moba", bp, seq)
            xp = mix_ffn(xp, [ha, hb], wo, gn_ffn, wg, wu, wd, tm_p)
            outs["p_mk"].append(kb.reshape(bp, seq, NH_B, DH_B))
            outs["p_mv"].append(vb.reshape(bp, seq, NH_B, DH_B))
            outs["p_c"].append(c1); outs["p_n"].append(n1); outs["p_m"].append(m1)

            a, gates, qb, kb, vb = norm_proj(xs, gn_mix, weights, auxes, kinds, tm_s)
            ha, c1, n1, m1 = _mlstm_group(a, gates, state_mlstm_c[e], state_mlstm_n[e], state_mlstm_m[e],
                                          g_out, bs, dec)
            hb = moba_sample(page_table, qb, kb, vb, cache_moba_k, cache_moba_v, e)
            xs = mix_ffn(xs, [ha, hb], wo, gn_ffn, wg, wu, wd, tm_s)
            outs["s_mk"].append(kb.reshape(bs, dec, NH_B, DH_B))
            outs["s_mv"].append(vb.reshape(bs, dec, NH_B, DH_B))
            outs["s_c"].append(c1); outs["s_n"].append(n1); outs["s_m"].append(m1)
        else:
            o = layer // 2
            lambda_init = 0.8 - 0.6 * math.exp(-0.3 * layer)
            w_in = w_in_odd[o].astype(BF16)
            weights = [w_in[:, i * D_MODEL:(i + 1) * D_MODEL] for i in range(3)]
            auxes = [_row(jnp.tile(g_q_diff[o].reshape(-1), NH_C)), _row(jnp.tile(g_k_diff[o].reshape(-1), NH_C)),
                     jnp.zeros((1, D_MODEL), F32)]
            kinds = ["half", "half", "none"]
            lam = lam_diff[o].astype(F32)
            gsub = _row(g_sub_diff[o])
            wo = w_out_odd[o].astype(BF16)

            q, k, v = norm_proj(xp, gn_mix, weights, auxes, kinds, tm_p)
            att = flash_prompt(q, k, v, [lam, gsub], "diff", bp, seq, lambda_init)
            xp = mix_ffn(xp, [att], wo, gn_ffn, wg, wu, wd, tm_p)
            outs["p_dk"].append(k.reshape(bp, seq, NH_C, 2 * DH_C))
            outs["p_dv"].append(v.reshape(bp, seq, NH_C, 2 * DH_C))

            q, k, v = norm_proj(xs, gn_mix, weights, auxes, kinds, tm_s)
            att = diff_sample(page_table, q, k, v, lam, gsub, cache_diff_k, cache_diff_v, o, lambda_init)
            xs = mix_ffn(xs, [att], wo, gn_ffn, wg, wu, wd, tm_s)
            outs["s_dk"].append(k.reshape(bs, dec, NH_C, 2 * DH_C))
            outs["s_dv"].append(v.reshape(bs, dec, NH_C, 2 * DH_C))

    st = {k: jnp.stack(v) for k, v in outs.items()}
    return (xp.reshape(bp, seq, D_MODEL), xs.reshape(bs, dec, D_MODEL),
            st["p_mk"], st["p_mv"], st["p_dk"], st["p_dv"], st["p_c"], st["p_n"], st["p_m"],
            st["s_mk"], st["s_mv"], st["s_dk"], st["s_dv"], st["s_c"], st["s_n"], st["s_m"])
```

```python
import functools
import math

import jax
import jax.numpy as jnp
from jax import lax
from jax.experimental import pallas as pl
from jax.experimental.pallas import tpu as pltpu

F32 = jnp.float32
BF16 = jnp.bfloat16

D_MODEL = 1024
NH_A, DH_A = 4, 128
W_A = NH_A * DH_A
NH_B, DH_B = 8, 64
W_B = NH_B * DH_B
NH_C, DH_C = 8, 64
MLSTM_CHUNK = 128
MOBA_BLOCK = 256
MOBA_TOPK = 3
PAGE = 128
EPS = 1e-6
NEG = -1e30
LANES = 128
FF_CHUNK = 256
VMEM_LIMIT = 56 * 1024 * 1024


def _dot(a, b):
    return jnp.dot(a, b, preferred_element_type=F32)


def _dot_nt(a, b):
    return lax.dot_general(a, b, (((1,), (1,)), ((), ())), preferred_element_type=F32)


def _rms(x, g):
    ms = jnp.mean(x * x, axis=-1, keepdims=True)
    return x * lax.rsqrt(ms + EPS) * g


def _half_norm(y, g):
    outs = []
    for c in range(y.shape[1] // LANES):
        yc = y[:, c * LANES:(c + 1) * LANES]
        sq = yc * yc
        lo = lax.broadcasted_iota(jnp.int32, yc.shape, 1) < 64
        s_lo = jnp.sum(jnp.where(lo, sq, 0.0), axis=-1, keepdims=True)
        s_hi = jnp.sum(jnp.where(lo, 0.0, sq), axis=-1, keepdims=True)
        ms = jnp.where(lo, s_lo, s_hi) * (1.0 / 64)
        outs.append(yc * lax.rsqrt(ms + EPS))
    return jnp.concatenate(outs, axis=-1) * g


def _params(sem):
    return pltpu.CompilerParams(dimension_semantics=sem, vmem_limit_bytes=VMEM_LIMIT)


def _resident(arr):
    nd = arr.ndim
    return pl.BlockSpec(arr.shape, lambda i: (0,) * nd, pipeline_mode=pl.Buffered(1))


def _form_shape(form, n_rows, width, tm):
    nb, tb = n_rows // MOBA_BLOCK, max(tm // MOBA_BLOCK, 1)
    if form == "f32":
        return (n_rows, width), F32, (tm, width), lambda i: (i, 0)
    if form in ("b16", "q16"):
        return (n_rows, width), BF16, (tm, width), lambda i: (i, 0)
    if form == "vt":
        return (nb, width // LANES, LANES, MOBA_BLOCK), BF16, (tb, width // LANES, LANES, MOBA_BLOCK), lambda i: (i, 0, 0, 0)
    assert form == "mean"
    return (nb, 1, width), F32, (tb, 1, width), lambda i: (i, 0, 0)


def _norm_proj_body(x_ref, gn_ref, *refs, kinds, forms):
    n = len(kinds)
    w_refs, aux_refs, out_refs = refs[:n], refs[n:2 * n], list(refs[2 * n:])
    h = _rms(x_ref[...], gn_ref[...]).astype(BF16)
    for idx, kind in enumerate(kinds):
        w_ref, aux_ref = w_refs[idx], aux_refs[idx]
        outs = {form: out_refs.pop(0) for form in forms[idx]}
        width = w_ref.shape[1]
        step = min(width, 512)
        for c0 in range(0, width, step):
            y = _dot(h, w_ref[:, c0:c0 + step])
            if kind == "bias":
                y = y + aux_ref[:, c0:c0 + step]
            elif kind == "half":
                y = _half_norm(y, aux_ref[:, c0:c0 + step])
            if "f32" in outs:
                outs["f32"][:, c0:c0 + step] = y
            if "b16" in outs:
                outs["b16"][:, c0:c0 + step] = y.astype(BF16)
            if "q16" in outs:
                outs["q16"][:, c0:c0 + step] = (y * (DH_B ** -0.5)).astype(BF16)
            for r in range(y.shape[0] // MOBA_BLOCK):
                blk = y[r * MOBA_BLOCK:(r + 1) * MOBA_BLOCK, :]
                if "mean" in outs:
                    outs["mean"][r, :, c0:c0 + step] = jnp.mean(blk, axis=0, keepdims=True)
                if "vt" in outs:
                    for g in range(step // LANES):
                        tile = jnp.transpose(blk[:, g * LANES:(g + 1) * LANES])
                        outs["vt"][r, c0 // LANES + g] = tile.astype(BF16)


def norm_proj(x, gn, weights, auxes, kinds, forms, tm):
    n_rows = x.shape[0]
    in_specs = [pl.BlockSpec((tm, D_MODEL), lambda i: (i, 0)), _resident(gn)]
    in_specs += [_resident(w) for w in weights] + [_resident(a) for a in auxes]
    out_shape, out_specs = [], []
    for w, fs in zip(weights, forms):
        for form in fs:
            shape, dtype, block, imap = _form_shape(form, n_rows, w.shape[1], tm)
            out_shape.append(jax.ShapeDtypeStruct(shape, dtype))
            out_specs.append(pl.BlockSpec(block, imap))
    return pl.pallas_call(
        functools.partial(_norm_proj_body, kinds=tuple(kinds), forms=tuple(tuple(f) for f in forms)),
        grid=(n_rows // tm,), in_specs=in_specs, out_specs=out_specs, out_shape=out_shape,
        compiler_params=_params(("parallel",)), name="norm_proj",
    )(x, gn, *weights, *auxes)


def _mix_ffn_body(x_ref, *refs, n_mix):
    a_refs = refs[:n_mix]
    wo_ref, gn_ref, wg_ref, wu_ref, wd_ref, out_ref = refs[n_mix:]
    mixed = jnp.concatenate([a_ref[...] for a_ref in a_refs], axis=-1).astype(BF16)
    x1 = x_ref[...] + _dot(mixed, wo_ref[...])
    h = _rms(x1, gn_ref[...]).astype(BF16)
    acc = x1
    for c in range(wg_ref.shape[0]):
        g = _dot(h, wg_ref[c])
        u = _dot(h, wu_ref[c])
        act = (g * jax.nn.sigmoid(g) * u).astype(BF16)
        acc = acc + _dot(act, wd_ref[c])
    out_ref[...] = acc


def mix_ffn(x, mixes, wo, gn, wg, wu, wd, tm):
    n_rows = x.shape[0]
    in_specs = [pl.BlockSpec((tm, D_MODEL), lambda i: (i, 0))]
    in_specs += [pl.BlockSpec((tm, a.shape[1]), lambda i: (i, 0)) for a in mixes]
    in_specs += [_resident(w) for w in (wo, gn, wg, wu, wd)]
    return pl.pallas_call(
        functools.partial(_mix_ffn_body, n_mix=len(mixes)),
        grid=(n_rows // tm,), in_specs=in_specs,
        out_specs=pl.BlockSpec((tm, D_MODEL), lambda i: (i, 0)),
        out_shape=jax.ShapeDtypeStruct((n_rows, D_MODEL), F32),
        compiler_params=_params(("parallel",)), name="mix_ffn",
    )(x, *mixes, wo, gn, wg, wu, wd)


def _log_sigmoid(x):
    return jnp.minimum(x, 0.0) - jnp.log1p(jnp.exp(-jnp.abs(x)))


def _gate_prep_body(g_ref, col_ref, row_ref, *, valid_len):
    chunk = MLSTM_CHUNK
    for c in range(g_ref.shape[0] // chunk):
        g = g_ref[c * chunk:(c + 1) * chunk, :]
        lane = lax.broadcasted_iota(jnp.int32, g.shape, 1)
        row = lax.broadcasted_iota(jnp.int32, g.shape, 0)
        is_f = (lane >= NH_A) & (lane < 2 * NH_A)
        live = row < valid_len
        logf = jnp.where(is_f & live, _log_sigmoid(g), 0.0)
        k = 1
        while k < chunk:
            logf = logf + jnp.where(row >= k, pltpu.roll(logf, k, 0), 0.0)
            k *= 2
        out = jnp.where(is_f, logf, jnp.where(live, g, NEG))
        col_ref[c * chunk:(c + 1) * chunk, :] = out
        row_ref[:, c * chunk:(c + 1) * chunk] = jnp.transpose(out)[:8, :]


def gate_prep(gates, valid_len, tm):
    n_rows = gates.shape[0]
    return pl.pallas_call(
        functools.partial(_gate_prep_body, valid_len=valid_len),
        grid=(n_rows // tm,),
        in_specs=[pl.BlockSpec((tm, LANES), lambda i: (i, 0))],
        out_specs=[pl.BlockSpec((tm, LANES), lambda i: (i, 0)), pl.BlockSpec((8, tm), lambda i: (0, i))],
        out_shape=[jax.ShapeDtypeStruct((n_rows, LANES), F32), jax.ShapeDtypeStruct((8, n_rows), F32)],
        compiler_params=_params(("parallel",)), name="gate_prep",
    )(gates)


def _mlstm_body(q_ref, k_ref, v_ref, o_ref, gc_ref, gr_ref, c0_ref, n0_ref, m0_ref, gout_ref,
                h_ref, c_out, n_out, m_out, c_s, n_s, m_s):
    head = pl.program_id(1)
    chunk_id = pl.program_id(2)
    L = MLSTM_CHUNK

    @pl.when(chunk_id == 0)
    def _():
        c_s[...] = c0_ref[0]
        n_s[...] = n0_ref[0]
        m_s[...] = m0_ref[0]

    q = q_ref[...]
    k = k_ref[...] * (DH_A ** -0.5)
    v = v_ref[...]
    gc = gc_ref[...]
    gr = gr_ref[...]
    lane = lax.broadcasted_iota(jnp.int32, gc.shape, 1)
    sub = lax.broadcasted_iota(jnp.int32, gr.shape, 0)
    i_col = jnp.sum(jnp.where(lane == head, gc, 0.0), axis=-1, keepdims=True)
    b_col = jnp.sum(jnp.where(lane == head + NH_A, gc, 0.0), axis=-1, keepdims=True)
    i_row = jnp.sum(jnp.where(sub == head, gr, 0.0), axis=0, keepdims=True)
    b_row = jnp.sum(jnp.where(sub == head + NH_A, gr, 0.0), axis=0, keepdims=True)
    m_prev = m_s[:, 0:1]
    b_last = b_col[L - 1:L, :]
    c_prev = c_s[...]
    n_prev = n_s[...]

    tt = lax.broadcasted_iota(jnp.int32, (L, L), 0)
    ss = lax.broadcasted_iota(jnp.int32, (L, L), 1)
    log_d = jnp.where(ss <= tt, b_col - b_row + i_row, NEG)
    inter = b_col + m_prev
    m_t = jnp.maximum(inter, jnp.max(log_d, axis=-1, keepdims=True))
    w_intra = jnp.exp(log_d - m_t)
    w_inter = jnp.exp(inter - m_t)
    qb, kb, vb = q.astype(BF16), k.astype(BF16), v.astype(BF16)
    s = _dot_nt(qb, kb) * w_intra
    num = w_inter * _dot_nt(qb, c_prev.astype(BF16)) + _dot(s.astype(BF16), vb)
    den = w_inter * jnp.sum(q * n_prev, axis=-1, keepdims=True) + jnp.sum(s, axis=-1, keepdims=True)
    hh = num / jnp.maximum(jnp.abs(den), jnp.exp(-m_t))
    m_new = m_t[L - 1:L, :]
    w_c = jnp.exp(b_last + m_prev - m_new)
    w_s = jnp.exp(b_last - b_col + i_col - m_new)
    vw_t = jnp.transpose(v * w_s).astype(BF16)
    c_new = w_c * c_prev + _dot(vw_t, kb)
    n_new = w_c * n_prev + jnp.sum(w_s * k, axis=0, keepdims=True)
    c_s[...] = c_new
    n_s[...] = n_new
    m_s[...] = jnp.broadcast_to(m_new, m_s.shape)

    hn = hh * lax.rsqrt(jnp.mean(hh * hh, axis=-1, keepdims=True) + EPS) * gout_ref[...]
    h_ref[...] = hn * jax.nn.sigmoid(o_ref[...])
    c_out[0] = c_new
    n_out[0] = n_new
    m_out[0] = jnp.broadcast_to(m_new, m_s.shape)


def mlstm(a, g_col, g_row, c0, n0, m0, g_out, batch, n_chunks):
    L = MLSTM_CHUNK
    n_rows = a.shape[0]

    def col(off):
        return pl.BlockSpec((L, DH_A), lambda b, h, c: (b * n_chunks + c, off + h))

    def state(shape):
        return pl.BlockSpec((1,) + shape, lambda b, h, c: (b * NH_A + h, 0, 0))

    in_specs = [col(0), col(NH_A), col(2 * NH_A), col(3 * NH_A),
                pl.BlockSpec((L, LANES), lambda b, h, c: (b * n_chunks + c, 0)),
                pl.BlockSpec((8, L), lambda b, h, c: (0, b * n_chunks + c)),
                state((DH_A, DH_A)), state((1, DH_A)), state((1, LANES)),
                pl.BlockSpec((1, DH_A), lambda b, h, c: (0, 0))]
    out_specs = [pl.BlockSpec((L, DH_A), lambda b, h, c: (b * n_chunks + c, h)),
                 state((DH_A, DH_A)), state((1, DH_A)), state((1, LANES))]
    out_shape = [jax.ShapeDtypeStruct((n_rows, W_A), F32),
                 jax.ShapeDtypeStruct((batch * NH_A, DH_A, DH_A), F32),
                 jax.ShapeDtypeStruct((batch * NH_A, 1, DH_A), F32),
                 jax.ShapeDtypeStruct((batch * NH_A, 1, LANES), F32)]
    return pl.pallas_call(
        _mlstm_body, grid=(batch, NH_A, n_chunks), in_specs=in_specs, out_specs=out_specs,
        out_shape=out_shape,
        scratch_shapes=[pltpu.VMEM((DH_A, DH_A), F32), pltpu.VMEM((1, DH_A), F32), pltpu.VMEM((1, LANES), F32)],
        compiler_params=_params(("parallel", "parallel", "arbitrary")), name="mlstm",
    )(a, a, a, a, g_col, g_row, c0, n0, m0, g_out)


def _topk_bias(gate, valid):
    lane = lax.broadcasted_iota(jnp.int32, gate.shape, 1)
    g = jnp.where(valid, gate, NEG)
    chosen = jnp.zeros(gate.shape, jnp.bool_)
    for _ in range(MOBA_TOPK):
        top = jnp.max(g, axis=-1, keepdims=True)
        first = jnp.min(jnp.where(g == top, lane, LANES), axis=-1, keepdims=True)
        pick = lane == first
        chosen = chosen | pick
        g = jnp.where(pick, NEG, g)
    return jnp.where(chosen & valid, 0.0, NEG)


def _lambda(lam_ref, lambda_init):
    lp = lam_ref[...]
    s01 = jnp.sum(lp[0:1, :] * lp[1:2, :], axis=-1, keepdims=True)
    s23 = jnp.sum(lp[2:3, :] * lp[3:4, :], axis=-1, keepdims=True)
    return jnp.exp(s01) - jnp.exp(s23) + lambda_init


def _topk_bias_rows(gate, valid):
    row = lax.broadcasted_iota(jnp.int32, gate.shape, 0)
    g = jnp.where(valid, gate, NEG)
    chosen = jnp.zeros(gate.shape, jnp.bool_)
    for _ in range(MOBA_TOPK):
        top = jnp.max(g, axis=0, keepdims=True)
        first = jnp.min(jnp.where(g == top, row, gate.shape[0]), axis=0, keepdims=True)
        pick = row == first
        chosen = chosen | pick
        g = jnp.where(pick, NEG, g)
    return jnp.where(chosen & valid, 0.0, NEG)


def _flash_body(*refs, mode, tq, lambda_init):
    if mode == "moba":
        q_ref, k_ref, vt_ref, km_ref, o_ref, m_s, l_s, acc_s, sel_s = refs
    else:
        q_ref, k_ref, vt_ref, lam_ref, gsub_ref, o_ref, m_s, l_s, acc_s = refs
    i = pl.program_id(2)
    lo = lax.broadcasted_iota(jnp.int32, (tq, LANES), 1) < 64
    q = q_ref[...]
    zero = jnp.zeros_like(q)
    q2 = jnp.concatenate([jnp.where(lo, q, zero), jnp.where(lo, zero, q)], axis=0)

    def scores(j):
        start = pl.multiple_of(j * tq, tq)
        return _dot_nt(k_ref[pl.ds(start, tq), :], q2)

    key = lax.broadcasted_iota(jnp.int32, (tq, 2 * tq), 0)
    col = lax.broadcasted_iota(jnp.int32, (tq, 2 * tq), 1)
    s = jnp.where(key <= jnp.where(col >= tq, col - tq, col), scores(i), NEG)
    m0 = jnp.max(s, axis=0, keepdims=True)
    p = jnp.exp(s - m0)
    m_s[...] = m0
    l_s[...] = jnp.sum(p, axis=0, keepdims=True)
    acc_s[...] = _dot(vt_ref[i, 0], p.astype(BF16))

    if mode == "moba":
        gate = _dot_nt(km_ref[0].astype(BF16), q2)
        sel_s[...] = _topk_bias_rows(gate, lax.broadcasted_iota(jnp.int32, gate.shape, 0) < i)

    def body(j, carry):
        sj = scores(j)
        if mode == "moba":
            sj = sj + sel_s[pl.ds(j, 1), :]
        m_old = m_s[...]
        m_new = jnp.maximum(m_old, jnp.max(sj, axis=0, keepdims=True))
        alpha = jnp.exp(m_old - m_new)
        pj = jnp.exp(sj - m_new)
        l_s[...] = alpha * l_s[...] + jnp.sum(pj, axis=0, keepdims=True)
        acc_s[...] = alpha * acc_s[...] + _dot(vt_ref[j, 0], pj.astype(BF16))
        m_s[...] = m_new
        return carry

    lax.fori_loop(0, i, body, 0)

    o2t = acc_s[...] / l_s[...]
    c1 = jnp.transpose(o2t[:, :tq])
    c2 = jnp.transpose(o2t[:, tq:])
    if mode == "moba":
        o_ref[...] = jnp.where(lo, c1, c2)
    else:
        o = c1 - _lambda(lam_ref, lambda_init) * c2
        on = o * lax.rsqrt(jnp.mean(o * o, axis=-1, keepdims=True) + EPS) * gsub_ref[...]
        o_ref[...] = on * (1.0 - lambda_init)


def flash_prompt(q, k, vt, extras, mode, batch, seq, lambda_init=0.0):
    tq = MOBA_BLOCK
    groups = q.shape[1] // LANES
    nq = seq // tq
    in_specs = [pl.BlockSpec((tq, LANES), lambda b, g, i: (b * nq + i, g)),
                pl.BlockSpec((seq, LANES), lambda b, g, i: (b, g)),
                pl.BlockSpec((nq, 1, LANES, tq), lambda b, g, i: (b, g, 0, 0))]
    scratch = [pltpu.VMEM((1, 2 * tq), F32), pltpu.VMEM((1, 2 * tq), F32), pltpu.VMEM((LANES, 2 * tq), F32)]
    if mode == "moba":
        in_specs.append(pl.BlockSpec((1, nq, LANES), lambda b, g, i: (b, 0, g)))
        scratch.append(pltpu.VMEM((nq, 2 * tq), F32))
    else:
        in_specs += [pl.BlockSpec((4, DH_C), lambda b, g, i: (0, 0)), pl.BlockSpec((1, LANES), lambda b, g, i: (0, 0))]
    return pl.pallas_call(
        functools.partial(_flash_body, mode=mode, tq=tq, lambda_init=lambda_init),
        grid=(batch, groups, nq), in_specs=in_specs,
        out_specs=pl.BlockSpec((tq, LANES), lambda b, g, i: (b * nq + i, g)),
        out_shape=jax.ShapeDtypeStruct(q.shape, F32), scratch_shapes=scratch,
        compiler_params=_params(("parallel", "parallel", "arbitrary")), name="flash_" + mode,
    )(q, k, vt, *extras)


PAGES_PER_STEP = 4


def _page_specs(layer, page_shape):
    def spec(r):
        return pl.BlockSpec((1, 1) + page_shape,
                            lambda b, s, pt: (layer, pt[b, PAGES_PER_STEP * s + r]) + (0,) * len(page_shape))
    return [spec(r) for r in range(PAGES_PER_STEP)]


def _moba_sample_body(pt_ref, q_ref, kn_ref, vn_ref, *refs, t_new):
    k_refs, v_refs = refs[:PAGES_PER_STEP], refs[PAGES_PER_STEP:2 * PAGES_PER_STEP]
    o_ref, part_s, m_s, l_s, g_s = refs[2 * PAGES_PER_STEP:]
    step = pl.program_id(1)
    n_steps = pl.num_programs(1)
    rows = NH_B * t_new
    pages_per_block = MOBA_BLOCK // PAGE
    blocks_per_step = PAGES_PER_STEP // pages_per_block
    lane = lax.broadcasted_iota(jnp.int32, (rows, LANES), 1)

    @pl.when(step == 0)
    def _():
        m_s[...] = jnp.full(m_s.shape, NEG, F32)
        l_s[...] = jnp.zeros(l_s.shape, F32)
        g_s[...] = jnp.full(g_s.shape, NEG, F32)

    qs = q_ref[...] * (DH_B ** -0.5)
    q_heads = [qs[:, h * DH_B:(h + 1) * DH_B].astype(BF16) for h in range(NH_B)]

    def block_partial(score, weighted_values, mask):
        s = jnp.concatenate([score(h) for h in range(NH_B)], axis=0)
        gate = jnp.sum(s, axis=-1, keepdims=True)
        if mask is not None:
            s = jnp.where(mask, s, NEG)
        m = jnp.max(s, axis=-1, keepdims=True)
        p = jnp.exp(s - m)
        l = jnp.sum(p, axis=-1, keepdims=True)
        o = jnp.concatenate([weighted_values(h, p[h * t_new:(h + 1) * t_new].astype(BF16))
                             for h in range(NH_B)], axis=0)
        return m, l, gate, o

    for r in range(blocks_per_step):
        pages = range(r * pages_per_block, (r + 1) * pages_per_block)

        def score(h, pages=pages):
            kt = jnp.concatenate([k_refs[p][0, 0, h] for p in pages], axis=1).astype(BF16)
            return _dot(q_heads[h], kt)

        def weighted_values(h, ph, pages=pages):
            vt = jnp.concatenate([v_refs[p][0, 0, h] for p in pages], axis=1).astype(BF16)
            return _dot_nt(ph, vt)

        m, l, gate, o = block_partial(score, weighted_values, None)
        j = step * blocks_per_step + r
        part_s[j] = o
        here = lane == j
        m_s[...] = jnp.where(here, m, m_s[...])
        l_s[...] = jnp.where(here, l, l_s[...])
        g_s[...] = jnp.where(here, gate, g_s[...])

    @pl.when(step == n_steps - 1)
    def _():
        n_past = n_steps * blocks_per_step
        pad = jnp.zeros((LANES - t_new, W_B), F32)
        kn = jnp.concatenate([kn_ref[...], pad], axis=0)
        vn = jnp.concatenate([vn_ref[...], pad], axis=0)
        rr = lax.broadcasted_iota(jnp.int32, (rows, LANES), 0)
        causal = lane <= rr % t_new
        m_own, l_own, _, o_own = block_partial(
            lambda h: _dot_nt(q_heads[h], kn[:, h * DH_B:(h + 1) * DH_B].astype(BF16)),
            lambda h, ph: _dot(ph, vn[:, h * DH_B:(h + 1) * DH_B].astype(BF16)), causal)
        bias = _topk_bias(g_s[...], lane < n_past)
        m_all = jnp.maximum(jnp.max(m_s[...] + bias, axis=-1, keepdims=True), m_own)
        w = jnp.exp(m_s[...] + bias - m_all)
        w_own = jnp.exp(m_own - m_all)
        den = jnp.sum(w * l_s[...], axis=-1, keepdims=True) + w_own * l_own
        acc = w_own * o_own
        for j in range(n_past):
            acc = acc + w[:, j:j + 1] * part_s[j]
        out = acc / den
        o_ref[...] = jnp.concatenate([out[h * t_new:(h + 1) * t_new] for h in range(NH_B)], axis=-1)


def moba_sample(page_table, qn, kn, vn, cache_k, cache_v, layer):
    batch, n_pages = page_table.shape
    t_new = qn.shape[0] // batch
    n_steps = n_pages // PAGES_PER_STEP
    n_blocks = n_pages * PAGE // MOBA_BLOCK
    rows = NH_B * t_new
    new = pl.BlockSpec((t_new, W_B), lambda b, s, pt: (b, 0))
    grid_spec = pltpu.PrefetchScalarGridSpec(
        num_scalar_prefetch=1, grid=(batch, n_steps),
        in_specs=[new, new, new] + 2 * _page_specs(layer, (NH_B, DH_B, PAGE)),
        out_specs=new,
        scratch_shapes=[pltpu.VMEM((n_blocks, rows, DH_B), F32), pltpu.VMEM((rows, LANES), F32),
                        pltpu.VMEM((rows, LANES), F32), pltpu.VMEM((rows, LANES), F32)])
    return pl.pallas_call(
        functools.partial(_moba_sample_body, t_new=t_new), grid_spec=grid_spec,
        out_shape=jax.ShapeDtypeStruct(qn.shape, F32),
        compiler_params=_params(("parallel", "arbitrary")), name="moba_sample",
    )(page_table, qn, kn, vn, *([cache_k] * PAGES_PER_STEP), *([cache_v] * PAGES_PER_STEP))


def _diff_sample_body(pt_ref, q_ref, kn_ref, vn_ref, lam_ref, gsub_ref, *refs, t_new, lambda_init):
    k_refs, v_refs = refs[:PAGES_PER_STEP], refs[PAGES_PER_STEP:2 * PAGES_PER_STEP]
    o_ref, m_s, l_s, acc_s = refs[2 * PAGES_PER_STEP:]
    step = pl.program_id(1)
    n_steps = pl.num_programs(1)
    per_head = 2 * t_new
    rows = NH_C * per_head

    @pl.when(step == 0)
    def _():
        m_s[...] = jnp.full(m_s.shape, NEG, F32)
        l_s[...] = jnp.zeros(l_s.shape, F32)
        acc_s[...] = jnp.zeros(acc_s.shape, F32)

    qs = q_ref[...] * (DH_C ** -0.5)
    lo = lax.broadcasted_iota(jnp.int32, (t_new, LANES), 1) < DH_C
    q_heads = []
    for h in range(NH_C):
        qh = qs[:, h * LANES:(h + 1) * LANES]
        q_heads.append(jnp.concatenate([jnp.where(lo, qh, 0.0), jnp.where(lo, 0.0, qh)], axis=0).astype(BF16))

    def update(k_of_head, v_of_head, mask):
        s = jnp.concatenate([_dot_nt(q_heads[h], k_of_head(h)) for h in range(NH_C)], axis=0)
        if mask is not None:
            s = jnp.where(mask, s, NEG)
        m_old = m_s[...]
        m_new = jnp.maximum(m_old, jnp.max(s, axis=-1, keepdims=True))
        alpha = jnp.exp(m_old - m_new)
        p = jnp.exp(s - m_new)
        l_s[...] = alpha * l_s[...] + jnp.sum(p, axis=-1, keepdims=True)
        pv = jnp.concatenate([_dot(p[h * per_head:(h + 1) * per_head].astype(BF16), v_of_head(h))
                              for h in range(NH_C)], axis=0)
        acc_s[...] = alpha * acc_s[...] + pv
        m_s[...] = m_new

    def head_rows(page_refs, h):
        rows_h = [r[0, 0, pl.ds(h, PAGE, stride=NH_C), :] for r in page_refs]
        return jnp.concatenate(rows_h, axis=0).astype(BF16)

    update(lambda h: head_rows(k_refs, h), lambda h: head_rows(v_refs, h), None)

    @pl.when(step == n_steps - 1)
    def _():
        pad = jnp.zeros((LANES - t_new, NH_C * LANES), F32)
        kn = jnp.concatenate([kn_ref[...], pad], axis=0)
        vn = jnp.concatenate([vn_ref[...], pad], axis=0)
        rr = lax.broadcasted_iota(jnp.int32, (rows, LANES), 0)
        cc = lax.broadcasted_iota(jnp.int32, (rows, LANES), 1)
        update(lambda h: kn[:, h * LANES:(h + 1) * LANES].astype(BF16),
               lambda h: vn[:, h * LANES:(h + 1) * LANES].astype(BF16), cc <= rr % t_new)
        o2 = acc_s[...] / l_s[...]
        lam = _lambda(lam_ref, lambda_init)
        outs = []
        for h in range(NH_C):
            base = h * per_head
            o = o2[base:base + t_new] - lam * o2[base + t_new:base + per_head]
            on = o * lax.rsqrt(jnp.mean(o * o, axis=-1, keepdims=True) + EPS) * gsub_ref[...]
            outs.append(on * (1.0 - lambda_init))
        o_ref[...] = jnp.concatenate(outs, axis=-1)


def diff_sample(page_table, qn, kn, vn, lam, gsub, cache_k, cache_v, layer, lambda_init):
    batch, n_pages = page_table.shape
    t_new = qn.shape[0] // batch
    n_steps = n_pages // PAGES_PER_STEP
    rows = NH_C * 2 * t_new
    new = pl.BlockSpec((t_new, NH_C * LANES), lambda b, s, pt: (b, 0))
    grid_spec = pltpu.PrefetchScalarGridSpec(
        num_scalar_prefetch=1, grid=(batch, n_steps),
        in_specs=[new, new, new,
                  pl.BlockSpec((4, DH_C), lambda b, s, pt: (0, 0)),
                  pl.BlockSpec((1, LANES), lambda b, s, pt: (0, 0))]
        + 2 * _page_specs(layer, (PAGE * NH_C, LANES)),
        out_specs=new,
        scratch_shapes=[pltpu.VMEM((rows, 1), F32), pltpu.VMEM((rows, 1), F32), pltpu.VMEM((rows, LANES), F32)])
    return pl.pallas_call(
        functools.partial(_diff_sample_body, t_new=t_new, lambda_init=lambda_init), grid_spec=grid_spec,
        out_shape=jax.ShapeDtypeStruct(qn.shape, F32),
        compiler_params=_params(("parallel", "arbitrary")), name="diff_sample",
    )(page_table, qn, kn, vn, lam, gsub, *([cache_k] * PAGES_PER_STEP), *([cache_v] * PAGES_PER_STEP))


def _row(v):
    return v.reshape(1, -1).astype(F32)


def _even_weights(w_in, b_gate, g_q, g_k):
    w_a = w_in[:, :4 * W_A].astype(BF16)
    g0 = 4 * W_A
    w_g = jnp.pad(w_in[:, g0:g0 + 2 * NH_A], ((0, 0), (0, LANES - 2 * NH_A))).astype(BF16)
    b0 = g0 + 2 * NH_A
    w_q, w_k, w_v = (w_in[:, b0 + i * W_B:b0 + (i + 1) * W_B].astype(BF16) for i in range(3))
    bias = jnp.pad(_row(b_gate), ((0, 0), (0, LANES - 2 * NH_A)))
    zeros = jnp.zeros((1, W_B), F32)
    weights = [w_a, w_g, w_q, w_k, w_v]
    auxes = [jnp.zeros((1, 4 * W_A), F32), bias, _row(jnp.tile(g_q, NH_B)), _row(jnp.tile(g_k, NH_B)), zeros]
    return weights, auxes, ["none", "bias", "half", "half", "none"]


def _ffn_weights(w_gate, w_up, w_down):
    d_ff = w_gate.shape[1]
    n = d_ff // FF_CHUNK
    wg = w_gate.astype(BF16).reshape(D_MODEL, n, FF_CHUNK).transpose(1, 0, 2)
    wu = w_up.astype(BF16).reshape(D_MODEL, n, FF_CHUNK).transpose(1, 0, 2)
    wd = w_down.astype(BF16).reshape(n, FF_CHUNK, D_MODEL)
    return wg, wu, wd


def _mlstm_group(a, gates, c0, n0, m0, g_out, batch, seq):
    L = MLSTM_CHUNK
    if seq % L:
        a = jnp.pad(a.reshape(batch, seq, -1), ((0, 0), (0, L - seq), (0, 0))).reshape(batch * L, -1)
        gates = jnp.pad(gates.reshape(batch, seq, -1), ((0, 0), (0, L - seq), (0, 0))).reshape(batch * L, -1)
        valid, n_chunks = seq, 1
    else:
        valid, n_chunks = L, seq // L
    g_col, g_row = gate_prep(gates, valid, min(512, gates.shape[0]))
    ha, c1, n1, m1 = mlstm(a, g_col, g_row, c0.reshape(batch * NH_A, DH_A, DH_A),
                           n0.reshape(batch * NH_A, 1, DH_A),
                           jnp.broadcast_to(m0.reshape(batch * NH_A, 1, 1), (batch * NH_A, 1, LANES)),
                           g_out, batch, n_chunks)
    if seq % L:
        ha = ha.reshape(batch, L, W_A)[:, :seq].reshape(batch * seq, W_A)
    return (ha, c1.reshape(batch, NH_A, DH_A, DH_A), n1.reshape(batch, NH_A, DH_A),
            m1[:, 0, 0].reshape(batch, NH_A))


def kernel(x_prompt, x_sample, cache_moba_k, cache_moba_v, cache_diff_k, cache_diff_v, state_mlstm_c, state_mlstm_n, state_mlstm_m, page_table, norm_mix, norm_ffn, w_in_even, b_gate_even, g_out_mlstm, g_q_moba, g_k_moba, w_out_even, w_in_odd, g_q_diff, g_k_diff, lam_diff, g_sub_diff, w_out_odd, w_gate, w_up, w_down):
    bp, seq, _ = x_prompt.shape
    bs, dec, _ = x_sample.shape
    n_p, n_s = bp * seq, bs * dec
    tm_p, tm_s = 512, n_s
    depth = norm_mix.shape[0]
    xp = x_prompt.reshape(n_p, D_MODEL)
    xs = x_sample.reshape(n_s, D_MODEL)
    c_init = jnp.zeros((bp, NH_A, DH_A, DH_A), F32)
    n_init = jnp.zeros((bp, NH_A, DH_A), F32)
    m_init = jnp.full((bp, NH_A), NEG, F32)
    n_kblocks = seq // MOBA_BLOCK
    moba_k_pages = cache_moba_k.transpose(0, 1, 3, 4, 2)
    moba_v_pages = cache_moba_v.transpose(0, 1, 3, 4, 2)
    diff_k_pages = cache_diff_k.reshape(cache_diff_k.shape[:2] + (PAGE * NH_C, 2 * DH_C))
    diff_v_pages = cache_diff_v.reshape(cache_diff_v.shape[:2] + (PAGE * NH_C, 2 * DH_C))

    outs = {k: [] for k in ("p_mk", "p_mv", "p_dk", "p_dv", "p_c", "p_n", "p_m",
                            "s_mk", "s_mv", "s_dk", "s_dv", "s_c", "s_n", "s_m")}
    for layer in range(depth):
        gn_mix = _row(norm_mix[layer])
        gn_ffn = _row(norm_ffn[layer])
        wg, wu, wd = _ffn_weights(w_gate[layer], w_up[layer], w_down[layer])
        if layer % 2 == 0:
            e = layer // 2
            weights, auxes, kinds = _even_weights(w_in_even[e], b_gate_even[e], g_q_moba[e], g_k_moba[e])
            g_out = _row(g_out_mlstm[e])
            wo = w_out_even[e].astype(BF16)

            forms_p = [["f32"], ["f32"], ["q16"], ["f32", "b16", "mean"], ["f32", "vt"]]
            a, gates, q16, kb, k16, kmean, vb, vt = norm_proj(xp, gn_mix, weights, auxes, kinds, forms_p, tm_p)
            ha, c1, n1, m1 = _mlstm_group(a, gates, c_init, n_init, m_init, g_out, bp, seq)
            hb = flash_prompt(q16, k16, vt, [kmean.reshape(bp, n_kblocks, W_B)], "moba", bp, seq)
            xp = mix_ffn(xp, [ha, hb], wo, gn_ffn, wg, wu, wd, tm_p)
            outs["p_mk"].append(kb.reshape(bp, seq, NH_B, DH_B))
            outs["p_mv"].append(vb.reshape(bp, seq, NH_B, DH_B))
            outs["p_c"].append(c1); outs["p_n"].append(n1); outs["p_m"].append(m1)

            a, gates, qb, kb, vb = norm_proj(xs, gn_mix, weights, auxes, kinds, [["f32"]] * 5, tm_s)
            ha, c1, n1, m1 = _mlstm_group(a, gates, state_mlstm_c[e], state_mlstm_n[e], state_mlstm_m[e],
                                          g_out, bs, dec)
            hb = moba_sample(page_table, qb, kb, vb, moba_k_pages, moba_v_pages, e)
            xs = mix_ffn(xs, [ha, hb], wo, gn_ffn, wg, wu, wd, tm_s)
            outs["s_mk"].append(kb.reshape(bs, dec, NH_B, DH_B))
            outs["s_mv"].append(vb.reshape(bs, dec, NH_B, DH_B))
            outs["s_c"].append(c1); outs["s_n"].append(n1); outs["s_m"].append(m1)
        else:
            o = layer // 2
            lambda_init = 0.8 - 0.6 * math.exp(-0.3 * layer)
            w_in = w_in_odd[o].astype(BF16)
            weights = [w_in[:, i * D_MODEL:(i + 1) * D_MODEL] for i in range(3)]
            auxes = [_row(jnp.tile(g_q_diff[o].reshape(-1), NH_C)), _row(jnp.tile(g_k_diff[o].reshape(-1), NH_C)),
                     jnp.zeros((1, D_MODEL), F32)]
            kinds = ["half", "half", "none"]
            lam = lam_diff[o].astype(F32)
            gsub = _row(g_sub_diff[o])
            wo = w_out_odd[o].astype(BF16)

            forms_p = [["q16"], ["f32", "b16"], ["f32", "vt"]]
            q16, k, k16, v, vt = norm_proj(xp, gn_mix, weights, auxes, kinds, forms_p, tm_p)
            att = flash_prompt(q16, k16, vt, [lam, gsub], "diff", bp, seq, lambda_init)
            xp = mix_ffn(xp, [att], wo, gn_ffn, wg, wu, wd, tm_p)
            outs["p_dk"].append(k.reshape(bp, seq, NH_C, 2 * DH_C))
            outs["p_dv"].append(v.reshape(bp, seq, NH_C, 2 * DH_C))

            q, k, v = norm_proj(xs, gn_mix, weights, auxes, kinds, [["f32"]] * 3, tm_s)
            att = diff_sample(page_table, q, k, v, lam, gsub, diff_k_pages, diff_v_pages, o, lambda_init)
            xs = mix_ffn(xs, [att], wo, gn_ffn, wg, wu, wd, tm_s)
            outs["s_dk"].append(k.reshape(bs, dec, NH_C, 2 * DH_C))
            outs["s_dv"].append(v.reshape(bs, dec, NH_C, 2 * DH_C))

    st = {k: jnp.stack(v) for k, v in outs.items()}
    return (xp.reshape(bp, seq, D_MODEL), xs.reshape(bs, dec, D_MODEL),
            st["p_mk"], st["p_mv"], st["p_dk"], st["p_dv"], st["p_c"], st["p_n"], st["p_m"],
            st["s_mk"], st["s_mv"], st["s_dk"], st["s_dv"], st["s_c"], st["s_n"], st["s_m"])
```

```python
import functools
import math

import jax
import jax.numpy as jnp
from jax import lax
from jax.experimental import pallas as pl
from jax.experimental.pallas import tpu as pltpu

F32 = jnp.float32
BF16 = jnp.bfloat16

D_MODEL = 1024
NH_A, DH_A = 4, 128
W_A = NH_A * DH_A
NH_B, DH_B = 8, 64
W_B = NH_B * DH_B
NH_C, DH_C = 8, 64
MLSTM_CHUNK = 128
MOBA_BLOCK = 256
MOBA_TOPK = 3
PAGE = 128
EPS = 1e-6
NEG = -1e30
LANES = 128
FF_CHUNK = 256
VMEM_LIMIT = 56 * 1024 * 1024


def _dot(a, b):
    return jnp.dot(a, b, preferred_element_type=F32)


def _dot_nt(a, b):
    return lax.dot_general(a, b, (((1,), (1,)), ((), ())), preferred_element_type=F32)


def _rms(x, g):
    ms = jnp.mean(x * x, axis=-1, keepdims=True)
    return x * lax.rsqrt(ms + EPS) * g


def _half_norm(y, g):
    outs = []
    for c in range(y.shape[1] // LANES):
        yc = y[:, c * LANES:(c + 1) * LANES]
        sq = yc * yc
        lo = lax.broadcasted_iota(jnp.int32, yc.shape, 1) < 64
        s_lo = jnp.sum(jnp.where(lo, sq, 0.0), axis=-1, keepdims=True)
        s_hi = jnp.sum(jnp.where(lo, 0.0, sq), axis=-1, keepdims=True)
        ms = jnp.where(lo, s_lo, s_hi) * (1.0 / 64)
        outs.append(yc * lax.rsqrt(ms + EPS))
    return jnp.concatenate(outs, axis=-1) * g


def _params(sem):
    return pltpu.CompilerParams(dimension_semantics=sem, vmem_limit_bytes=VMEM_LIMIT)


def _resident(arr):
    nd = arr.ndim
    return pl.BlockSpec(arr.shape, lambda i: (0,) * nd, pipeline_mode=pl.Buffered(1))


def _form_shape(form, n_rows, width, tm):
    nb, tb = n_rows // MOBA_BLOCK, max(tm // MOBA_BLOCK, 1)
    if form == "f32":
        return (n_rows, width), F32, (tm, width), lambda i: (i, 0)
    if form in ("b16", "q16"):
        return (n_rows, width), BF16, (tm, width), lambda i: (i, 0)
    if form == "vt":
        return (nb, width // LANES, LANES, MOBA_BLOCK), BF16, (tb, width // LANES, LANES, MOBA_BLOCK), lambda i: (i, 0, 0, 0)
    assert form == "mean"
    return (nb, 1, width), F32, (tb, 1, width), lambda i: (i, 0, 0)


def _norm_proj_body(x_ref, gn_ref, *refs, kinds, forms):
    n = len(kinds)
    w_refs, aux_refs, out_refs = refs[:n], refs[n:2 * n], list(refs[2 * n:])
    h = _rms(x_ref[...], gn_ref[...]).astype(BF16)
    for idx, kind in enumerate(kinds):
        w_ref, aux_ref = w_refs[idx], aux_refs[idx]
        outs = {form: out_refs.pop(0) for form in forms[idx]}
        width = w_ref.shape[1]
        step = min(width, 512)
        for c0 in range(0, width, step):
            y = _dot(h, w_ref[:, c0:c0 + step])
            if kind == "bias":
                y = y + aux_ref[:, c0:c0 + step]
            elif kind == "half":
                y = _half_norm(y, aux_ref[:, c0:c0 + step])
            if "f32" in outs:
                outs["f32"][:, c0:c0 + step] = y
            if "b16" in outs:
                outs["b16"][:, c0:c0 + step] = y.astype(BF16)
            if "q16" in outs:
                outs["q16"][:, c0:c0 + step] = (y * (DH_B ** -0.5)).astype(BF16)
            for r in range(y.shape[0] // MOBA_BLOCK):
                blk = y[r * MOBA_BLOCK:(r + 1) * MOBA_BLOCK, :]
                if "mean" in outs:
                    outs["mean"][r, :, c0:c0 + step] = jnp.mean(blk, axis=0, keepdims=True)
                if "vt" in outs:
                    for g in range(step // LANES):
                        tile = jnp.transpose(blk[:, g * LANES:(g + 1) * LANES])
                        outs["vt"][r, c0 // LANES + g] = tile.astype(BF16)


def norm_proj(x, gn, weights, auxes, kinds, forms, tm):
    n_rows = x.shape[0]
    in_specs = [pl.BlockSpec((tm, D_MODEL), lambda i: (i, 0)), _resident(gn)]
    in_specs += [_resident(w) for w in weights] + [_resident(a) for a in auxes]
    out_shape, out_specs = [], []
    for w, fs in zip(weights, forms):
        for form in fs:
            shape, dtype, block, imap = _form_shape(form, n_rows, w.shape[1], tm)
            out_shape.append(jax.ShapeDtypeStruct(shape, dtype))
            out_specs.append(pl.BlockSpec(block, imap))
    return pl.pallas_call(
        functools.partial(_norm_proj_body, kinds=tuple(kinds), forms=tuple(tuple(f) for f in forms)),
        grid=(n_rows // tm,), in_specs=in_specs, out_specs=out_specs, out_shape=out_shape,
        compiler_params=_params(("parallel",)), name="norm_proj",
    )(x, gn, *weights, *auxes)


def _mix_ffn_body(x_ref, *refs, n_mix):
    a_refs = refs[:n_mix]
    wo_ref, gn_ref, wg_ref, wu_ref, wd_ref, out_ref = refs[n_mix:]
    mixed = jnp.concatenate([a_ref[...] for a_ref in a_refs], axis=-1).astype(BF16)
    x1 = x_ref[...] + _dot(mixed, wo_ref[...])
    h = _rms(x1, gn_ref[...]).astype(BF16)
    acc = x1
    for c in range(wg_ref.shape[0]):
        g = _dot(h, wg_ref[c])
        u = _dot(h, wu_ref[c])
        act = (g * jax.nn.sigmoid(g) * u).astype(BF16)
        acc = acc + _dot(act, wd_ref[c])
    out_ref[...] = acc


def mix_ffn(x, mixes, wo, gn, wg, wu, wd, tm):
    n_rows = x.shape[0]
    in_specs = [pl.BlockSpec((tm, D_MODEL), lambda i: (i, 0))]
    in_specs += [pl.BlockSpec((tm, a.shape[1]), lambda i: (i, 0)) for a in mixes]
    in_specs += [_resident(w) for w in (wo, gn, wg, wu, wd)]
    return pl.pallas_call(
        functools.partial(_mix_ffn_body, n_mix=len(mixes)),
        grid=(n_rows // tm,), in_specs=in_specs,
        out_specs=pl.BlockSpec((tm, D_MODEL), lambda i: (i, 0)),
        out_shape=jax.ShapeDtypeStruct((n_rows, D_MODEL), F32),
        compiler_params=_params(("parallel",)), name="mix_ffn",
    )(x, *mixes, wo, gn, wg, wu, wd)


def _log_sigmoid(x):
    return jnp.minimum(x, 0.0) - jnp.log1p(jnp.exp(-jnp.abs(x)))


def _pad_rows(x, rows):
    if x.shape[0] == rows:
        return x
    return jnp.concatenate([x, jnp.zeros((rows - x.shape[0],) + x.shape[1:], x.dtype)], axis=0)


def _gate_prep_body(g_ref, col_ref, row_ref, *, valid_len):
    chunk = MLSTM_CHUNK
    for c in range(col_ref.shape[0] // chunk):
        g = _pad_rows(g_ref[c * valid_len:(c + 1) * valid_len, :], chunk)
        lane = lax.broadcasted_iota(jnp.int32, g.shape, 1)
        row = lax.broadcasted_iota(jnp.int32, g.shape, 0)
        is_f = (lane >= NH_A) & (lane < 2 * NH_A)
        live = row < valid_len
        logf = jnp.where(is_f & live, _log_sigmoid(g), 0.0)
        k = 1
        while k < chunk:
            logf = logf + jnp.where(row >= k, pltpu.roll(logf, k, 0), 0.0)
            k *= 2
        out = jnp.where(is_f, logf, jnp.where(live, g, NEG))
        col_ref[c * chunk:(c + 1) * chunk, :] = out
        row_ref[:, c * chunk:(c + 1) * chunk] = jnp.transpose(out)[:8, :]


def gate_prep(gates, valid_len):
    chunk = MLSTM_CHUNK
    n_chunks = gates.shape[0] // valid_len
    per_tile = 4 if (valid_len == chunk and n_chunks % 4 == 0) else 1
    return pl.pallas_call(
        functools.partial(_gate_prep_body, valid_len=valid_len),
        grid=(n_chunks // per_tile,),
        in_specs=[pl.BlockSpec((per_tile * valid_len, LANES), lambda i: (i, 0))],
        out_specs=[pl.BlockSpec((per_tile * chunk, LANES), lambda i: (i, 0)),
                   pl.BlockSpec((8, per_tile * chunk), lambda i: (0, i))],
        out_shape=[jax.ShapeDtypeStruct((n_chunks * chunk, LANES), F32),
                   jax.ShapeDtypeStruct((8, n_chunks * chunk), F32)],
        compiler_params=_params(("parallel",)), name="gate_prep",
    )(gates)


def _mlstm_body(q_ref, k_ref, v_ref, o_ref, gc_ref, gr_ref, c0_ref, n0_ref, m0_ref, gout_ref,
                h_ref, c_out, n_out, m_out, c_s, n_s, m_s):
    chunk_id = pl.program_id(1)
    L = MLSTM_CHUNK
    live = q_ref.shape[0]

    @pl.when(chunk_id == 0)
    def _():
        c_s[...] = c0_ref[...]
        n_s[...] = n0_ref[...]
        m_s[...] = m0_ref[...]

    gc = gc_ref[...]
    gr = gr_ref[...]
    lane = lax.broadcasted_iota(jnp.int32, gc.shape, 1)
    sub = lax.broadcasted_iota(jnp.int32, gr.shape, 0)
    tt = lax.broadcasted_iota(jnp.int32, (L, L), 0)
    ss = lax.broadcasted_iota(jnp.int32, (L, L), 1)
    for head in range(NH_A):
        cols = slice(head * DH_A, (head + 1) * DH_A)
        q = _pad_rows(q_ref[:, cols], L)
        k = _pad_rows(k_ref[:, cols], L) * (DH_A ** -0.5)
        v = _pad_rows(v_ref[:, cols], L)
        i_col = jnp.sum(jnp.where(lane == head, gc, 0.0), axis=-1, keepdims=True)
        b_col = jnp.sum(jnp.where(lane == head + NH_A, gc, 0.0), axis=-1, keepdims=True)
        i_row = jnp.sum(jnp.where(sub == head, gr, 0.0), axis=0, keepdims=True)
        b_row = jnp.sum(jnp.where(sub == head + NH_A, gr, 0.0), axis=0, keepdims=True)
        m_prev = m_s[head][:, 0:1]
        b_last = b_col[L - 1:L, :]
        c_prev = c_s[head]
        n_prev = n_s[head]

        log_d = jnp.where(ss <= tt, b_col - b_row + i_row, NEG)
        inter = b_col + m_prev
        m_t = jnp.maximum(inter, jnp.max(log_d, axis=-1, keepdims=True))
        w_intra = jnp.exp(log_d - m_t)
        w_inter = jnp.exp(inter - m_t)
        qb, kb, vb = q.astype(BF16), k.astype(BF16), v.astype(BF16)
        s = _dot_nt(qb, kb) * w_intra
        num = w_inter * _dot_nt(qb, c_prev.astype(BF16)) + _dot(s.astype(BF16), vb)
        den = w_inter * jnp.sum(q * n_prev, axis=-1, keepdims=True) + jnp.sum(s, axis=-1, keepdims=True)
        hh = num / jnp.maximum(jnp.abs(den), jnp.exp(-m_t))
        m_new = m_t[L - 1:L, :]
        w_c = jnp.exp(b_last + m_prev - m_new)
        w_s = jnp.exp(b_last - b_col + i_col - m_new)
        vw_t = jnp.transpose(v * w_s).astype(BF16)
        c_new = w_c * c_prev + _dot(vw_t, kb)
        n_new = w_c * n_prev + jnp.sum(w_s * k, axis=0, keepdims=True)
        m_row = jnp.broadcast_to(m_new, (1, LANES))
        c_s[head] = c_new
        n_s[head] = n_new
        m_s[head] = m_row

        hn = hh * lax.rsqrt(jnp.mean(hh * hh, axis=-1, keepdims=True) + EPS) * gout_ref[...]
        h_ref[:, cols] = hn[:live] * jax.nn.sigmoid(o_ref[:, cols])
        c_out[head] = c_new
        n_out[head] = n_new
        m_out[head] = m_row


def mlstm(a, g_col, g_row, c0, n0, m0, g_out, batch, n_chunks, live):
    L = MLSTM_CHUNK
    n_rows = a.shape[0]

    def part(p):
        return pl.BlockSpec((live, W_A), lambda b, c: (b * n_chunks + c, p))

    def state(shape):
        return pl.BlockSpec((NH_A,) + shape, lambda b, c: (b, 0, 0))

    in_specs = [part(0), part(1), part(2), part(3),
                pl.BlockSpec((L, LANES), lambda b, c: (b * n_chunks + c, 0)),
                pl.BlockSpec((8, L), lambda b, c: (0, b * n_chunks + c)),
                state((DH_A, DH_A)), state((1, DH_A)), state((1, LANES)),
                pl.BlockSpec((1, DH_A), lambda b, c: (0, 0))]
    out_specs = [part(0), state((DH_A, DH_A)), state((1, DH_A)), state((1, LANES))]
    out_shape = [jax.ShapeDtypeStruct((n_rows, W_A), F32),
                 jax.ShapeDtypeStruct((batch * NH_A, DH_A, DH_A), F32),
                 jax.ShapeDtypeStruct((batch * NH_A, 1, DH_A), F32),
                 jax.ShapeDtypeStruct((batch * NH_A, 1, LANES), F32)]
    return pl.pallas_call(
        _mlstm_body, grid=(batch, n_chunks), in_specs=in_specs, out_specs=out_specs,
        out_shape=out_shape,
        scratch_shapes=[pltpu.VMEM((NH_A, DH_A, DH_A), F32), pltpu.VMEM((NH_A, 1, DH_A), F32),
                        pltpu.VMEM((NH_A, 1, LANES), F32)],
        compiler_params=_params(("parallel", "arbitrary")), name="mlstm",
    )(a, a, a, a, g_col, g_row, c0, n0, m0, g_out)


def _topk_bias(gate, valid):
    lane = lax.broadcasted_iota(jnp.int32, gate.shape, 1)
    g = jnp.where(valid, gate, NEG)
    chosen = jnp.zeros(gate.shape, jnp.bool_)
    for _ in range(MOBA_TOPK):
        top = jnp.max(g, axis=-1, keepdims=True)
        first = jnp.min(jnp.where(g == top, lane, LANES), axis=-1, keepdims=True)
        pick = lane == first
        chosen = chosen | pick
        g = jnp.where(pick, NEG, g)
    return jnp.where(chosen & valid, 0.0, NEG)


def _lambda(lam_ref, lambda_init):
    lp = lam_ref[...]
    s01 = jnp.sum(lp[0:1, :] * lp[1:2, :], axis=-1, keepdims=True)
    s23 = jnp.sum(lp[2:3, :] * lp[3:4, :], axis=-1, keepdims=True)
    return jnp.exp(s01) - jnp.exp(s23) + lambda_init


def _topk_bias_rows(gate, valid):
    row = lax.broadcasted_iota(jnp.int32, gate.shape, 0)
    g = jnp.where(valid, gate, NEG)
    chosen = jnp.zeros(gate.shape, jnp.bool_)
    for _ in range(MOBA_TOPK):
        top = jnp.max(g, axis=0, keepdims=True)
        first = jnp.min(jnp.where(g == top, row, gate.shape[0]), axis=0, keepdims=True)
        pick = row == first
        chosen = chosen | pick
        g = jnp.where(pick, NEG, g)
    return jnp.where(chosen & valid, 0.0, NEG)


def _flash_body(*refs, mode, tq, lambda_init):
    if mode == "moba":
        q_ref, k_ref, vt_ref, km_ref, o_ref, m_s, l_s, acc_s, s_buf, sel_s = refs
    else:
        q_ref, k_ref, vt_ref, lam_ref, gsub_ref, o_ref, m_s, l_s, acc_s, s_buf = refs
    i = pl.program_id(2)
    lo = lax.broadcasted_iota(jnp.int32, (tq, LANES), 1) < 64
    q = q_ref[...]
    zero = jnp.zeros_like(q)
    q2 = jnp.concatenate([jnp.where(lo, q, zero), jnp.where(lo, zero, q)], axis=0)

    def scores(j):
        start = pl.multiple_of(j * tq, tq)
        return _dot_nt(k_ref[pl.ds(start, tq), :], q2)

    def absorb(j, sj):
        m_old = m_s[...]
        m_new = jnp.maximum(m_old, jnp.max(sj, axis=0, keepdims=True))
        alpha = jnp.exp(m_old - m_new)
        pj = jnp.exp(sj - m_new)
        l_s[...] = alpha * l_s[...] + jnp.sum(pj, axis=0, keepdims=True)
        acc_s[...] = alpha * acc_s[...] + _dot(vt_ref[j, 0], pj.astype(BF16))
        m_s[...] = m_new

    m_s[...] = jnp.full(m_s.shape, NEG, F32)
    l_s[...] = jnp.zeros(l_s.shape, F32)
    acc_s[...] = jnp.zeros(acc_s.shape, F32)
    if mode == "moba":
        gate = _dot_nt(km_ref[0].astype(BF16), q2)
        sel_s[...] = _topk_bias_rows(gate, lax.broadcasted_iota(jnp.int32, gate.shape, 0) < i)

    s_buf[...] = scores(0)

    def body(j, carry):
        sj = s_buf[...]
        s_buf[...] = scores(j + 1)
        if mode == "moba":
            sj = sj + sel_s[pl.ds(j, 1), :]
        absorb(j, sj)
        return carry

    lax.fori_loop(0, i, body, 0)

    key = lax.broadcasted_iota(jnp.int32, (tq, 2 * tq), 0)
    col = lax.broadcasted_iota(jnp.int32, (tq, 2 * tq), 1)
    absorb(i, jnp.where(key <= jnp.where(col >= tq, col - tq, col), s_buf[...], NEG))

    o2t = acc_s[...] / l_s[...]
    c1 = jnp.transpose(o2t[:, :tq])
    c2 = jnp.transpose(o2t[:, tq:])
    if mode == "moba":
        o_ref[...] = jnp.where(lo, c1, c2)
    else:
        o = c1 - _lambda(lam_ref, lambda_init) * c2
        on = o * lax.rsqrt(jnp.mean(o * o, axis=-1, keepdims=True) + EPS) * gsub_ref[...]
        o_ref[...] = on * (1.0 - lambda_init)


def flash_prompt(q, k, vt, extras, mode, batch, seq, lambda_init=0.0):
    tq = MOBA_BLOCK
    groups = q.shape[1] // LANES
    nq = seq // tq
    in_specs = [pl.BlockSpec((tq, LANES), lambda b, g, i: (b * nq + i, g)),
                pl.BlockSpec((seq, LANES), lambda b, g, i: (b, g)),
                pl.BlockSpec((nq, 1, LANES, tq), lambda b, g, i: (b, g, 0, 0))]
    scratch = [pltpu.VMEM((1, 2 * tq), F32), pltpu.VMEM((1, 2 * tq), F32), pltpu.VMEM((LANES, 2 * tq), F32),
               pltpu.VMEM((tq, 2 * tq), F32)]
    if mode == "moba":
        in_specs.append(pl.BlockSpec((1, nq, LANES), lambda b, g, i: (b, 0, g)))
        scratch.append(pltpu.VMEM((nq, 2 * tq), F32))
    else:
        in_specs += [pl.BlockSpec((4, DH_C), lambda b, g, i: (0, 0)), pl.BlockSpec((1, LANES), lambda b, g, i: (0, 0))]
    return pl.pallas_call(
        functools.partial(_flash_body, mode=mode, tq=tq, lambda_init=lambda_init),
        grid=(batch, groups, nq), in_specs=in_specs,
        out_specs=pl.BlockSpec((tq, LANES), lambda b, g, i: (b * nq + i, g)),
        out_shape=jax.ShapeDtypeStruct(q.shape, F32), scratch_shapes=scratch,
        compiler_params=_params(("parallel", "parallel", "arbitrary")), name="flash_" + mode,
    )(q, k, vt, *extras)


PAGES_PER_STEP = 8


def _page_specs(layer, page_shape):
    def spec(r):
        return pl.BlockSpec((1, 1) + page_shape,
                            lambda b, s, pt: (layer, pt[b, PAGES_PER_STEP * s + r]) + (0,) * len(page_shape))
    return [spec(r) for r in range(PAGES_PER_STEP)]


def _moba_sample_body(pt_ref, q_ref, kn_ref, vn_ref, *refs, t_new):
    k_refs, v_refs = refs[:PAGES_PER_STEP], refs[PAGES_PER_STEP:2 * PAGES_PER_STEP]
    o_ref, part_s, m_s, l_s, g_s = refs[2 * PAGES_PER_STEP:]
    step = pl.program_id(1)
    n_steps = pl.num_programs(1)
    rows = NH_B * t_new
    pages_per_block = MOBA_BLOCK // PAGE
    blocks_per_step = PAGES_PER_STEP // pages_per_block
    lane = lax.broadcasted_iota(jnp.int32, (rows, LANES), 1)

    @pl.when(step == 0)
    def _():
        m_s[...] = jnp.full(m_s.shape, NEG, F32)
        l_s[...] = jnp.zeros(l_s.shape, F32)
        g_s[...] = jnp.full(g_s.shape, NEG, F32)

    qs = q_ref[...] * (DH_B ** -0.5)
    q_heads = [qs[:, h * DH_B:(h + 1) * DH_B].astype(BF16) for h in range(NH_B)]

    def block_partial(score, weighted_values, mask):
        s = jnp.concatenate([score(h) for h in range(NH_B)], axis=0)
        gate = jnp.sum(s, axis=-1, keepdims=True)
        if mask is not None:
            s = jnp.where(mask, s, NEG)
        m = jnp.max(s, axis=-1, keepdims=True)
        p = jnp.exp(s - m)
        l = jnp.sum(p, axis=-1, keepdims=True)
        o = jnp.concatenate([weighted_values(h, p[h * t_new:(h + 1) * t_new].astype(BF16))
                             for h in range(NH_B)], axis=0)
        return m, l, gate, o

    for r in range(blocks_per_step):
        pages = range(r * pages_per_block, (r + 1) * pages_per_block)

        def score(h, pages=pages):
            kt = jnp.concatenate([k_refs[p][0, 0, h] for p in pages], axis=1).astype(BF16)
            return _dot(q_heads[h], kt)

        def weighted_values(h, ph, pages=pages):
            vt = jnp.concatenate([v_refs[p][0, 0, h] for p in pages], axis=1).astype(BF16)
            return _dot_nt(ph, vt)

        m, l, gate, o = block_partial(score, weighted_values, None)
        j = step * blocks_per_step + r
        part_s[j] = o
        here = lane == j
        m_s[...] = jnp.where(here, m, m_s[...])
        l_s[...] = jnp.where(here, l, l_s[...])
        g_s[...] = jnp.where(here, gate, g_s[...])

    @pl.when(step == n_steps - 1)
    def _():
        n_past = n_steps * blocks_per_step
        pad = jnp.zeros((LANES - t_new, W_B), F32)
        kn = jnp.concatenate([kn_ref[...], pad], axis=0)
        vn = jnp.concatenate([vn_ref[...], pad], axis=0)
        rr = lax.broadcasted_iota(jnp.int32, (rows, LANES), 0)
        causal = lane <= rr % t_new
        m_own, l_own, _, o_own = block_partial(
            lambda h: _dot_nt(q_heads[h], kn[:, h * DH_B:(h + 1) * DH_B].astype(BF16)),
            lambda h, ph: _dot(ph, vn[:, h * DH_B:(h + 1) * DH_B].astype(BF16)), causal)
        bias = _topk_bias(g_s[...], lane < n_past)
        m_all = jnp.maximum(jnp.max(m_s[...] + bias, axis=-1, keepdims=True), m_own)
        w = jnp.exp(m_s[...] + bias - m_all)
        w_own = jnp.exp(m_own - m_all)
        den = jnp.sum(w * l_s[...], axis=-1, keepdims=True) + w_own * l_own
        acc = w_own * o_own
        for j in range(n_past):
            acc = acc + w[:, j:j + 1] * part_s[j]
        out = acc / den
        o_ref[...] = jnp.concatenate([out[h * t_new:(h + 1) * t_new] for h in range(NH_B)], axis=-1)


def moba_sample(page_table, qn, kn, vn, cache_k, cache_v, layer):
    batch, n_pages = page_table.shape
    t_new = qn.shape[0] // batch
    n_steps = n_pages // PAGES_PER_STEP
    n_blocks = n_pages * PAGE // MOBA_BLOCK
    rows = NH_B * t_new
    new = pl.BlockSpec((t_new, W_B), lambda b, s, pt: (b, 0))
    grid_spec = pltpu.PrefetchScalarGridSpec(
        num_scalar_prefetch=1, grid=(batch, n_steps),
        in_specs=[new, new, new] + 2 * _page_specs(layer, (NH_B, DH_B, PAGE)),
        out_specs=new,
        scratch_shapes=[pltpu.VMEM((n_blocks, rows, DH_B), F32), pltpu.VMEM((rows, LANES), F32),
                        pltpu.VMEM((rows, LANES), F32), pltpu.VMEM((rows, LANES), F32)])
    return pl.pallas_call(
        functools.partial(_moba_sample_body, t_new=t_new), grid_spec=grid_spec,
        out_shape=jax.ShapeDtypeStruct(qn.shape, F32),
        compiler_params=_params(("parallel", "arbitrary")), name="moba_sample",
    )(page_table, qn, kn, vn, *([cache_k] * PAGES_PER_STEP), *([cache_v] * PAGES_PER_STEP))


def _diff_sample_body(pt_ref, q_ref, kn_ref, vn_ref, lam_ref, gsub_ref, *refs, t_new, lambda_init):
    k_refs, v_refs = refs[:PAGES_PER_STEP], refs[PAGES_PER_STEP:2 * PAGES_PER_STEP]
    o_ref, m_s, l_s, acc_s = refs[2 * PAGES_PER_STEP:]
    step = pl.program_id(1)
    n_steps = pl.num_programs(1)
    per_head = 2 * t_new
    rows = NH_C * per_head

    @pl.when(step == 0)
    def _():
        m_s[...] = jnp.full(m_s.shape, NEG, F32)
        l_s[...] = jnp.zeros(l_s.shape, F32)
        acc_s[...] = jnp.zeros(acc_s.shape, F32)

    qs = q_ref[...] * (DH_C ** -0.5)
    lo = lax.broadcasted_iota(jnp.int32, (t_new, LANES), 1) < DH_C
    q_heads = []
    for h in range(NH_C):
        qh = qs[:, h * LANES:(h + 1) * LANES]
        q_heads.append(jnp.concatenate([jnp.where(lo, qh, 0.0), jnp.where(lo, 0.0, qh)], axis=0).astype(BF16))

    def update(k_of_head, v_of_head, mask):
        s = jnp.concatenate([_dot_nt(q_heads[h], k_of_head(h)) for h in range(NH_C)], axis=0)
        if mask is not None:
            s = jnp.where(mask, s, NEG)
        m_old = m_s[...]
        m_new = jnp.maximum(m_old, jnp.max(s, axis=-1, keepdims=True))
        alpha = jnp.exp(m_old - m_new)
        p = jnp.exp(s - m_new)
        l_s[...] = alpha * l_s[...] + jnp.sum(p, axis=-1, keepdims=True)
        pv = jnp.concatenate([_dot(p[h * per_head:(h + 1) * per_head].astype(BF16), v_of_head(h))
                              for h in range(NH_C)], axis=0)
        acc_s[...] = alpha * acc_s[...] + pv
        m_s[...] = m_new

    def head_rows(page_refs, h):
        rows_h = [r[0, 0, pl.ds(h, PAGE, stride=NH_C), :] for r in page_refs]
        return jnp.concatenate(rows_h, axis=0).astype(BF16)

    update(lambda h: head_rows(k_refs, h), lambda h: head_rows(v_refs, h), None)

    @pl.when(step == n_steps - 1)
    def _():
        pad = jnp.zeros((LANES - t_new, NH_C * LANES), F32)
        kn = jnp.concatenate([kn_ref[...], pad], axis=0)
        vn = jnp.concatenate([vn_ref[...], pad], axis=0)
        rr = lax.broadcasted_iota(jnp.int32, (rows, LANES), 0)
        cc = lax.broadcasted_iota(jnp.int32, (rows, LANES), 1)
        update(lambda h: kn[:, h * LANES:(h + 1) * LANES].astype(BF16),
               lambda h: vn[:, h * LANES:(h + 1) * LANES].astype(BF16), cc <= rr % t_new)
        o2 = acc_s[...] / l_s[...]
        lam = _lambda(lam_ref, lambda_init)
        outs = []
        for h in range(NH_C):
            base = h * per_head
            o = o2[base:base + t_new] - lam * o2[base + t_new:base + per_head]
            on = o * lax.rsqrt(jnp.mean(o * o, axis=-1, keepdims=True) + EPS) * gsub_ref[...]
            outs.append(on * (1.0 - lambda_init))
        o_ref[...] = jnp.concatenate(outs, axis=-1)


def diff_sample(page_table, qn, kn, vn, lam, gsub, cache_k, cache_v, layer, lambda_init):
    batch, n_pages = page_table.shape
    t_new = qn.shape[0] // batch
    n_steps = n_pages // PAGES_PER_STEP
    rows = NH_C * 2 * t_new
    new = pl.BlockSpec((t_new, NH_C * LANES), lambda b, s, pt: (b, 0))
    grid_spec = pltpu.PrefetchScalarGridSpec(
        num_scalar_prefetch=1, grid=(batch, n_steps),
        in_specs=[new, new, new,
                  pl.BlockSpec((4, DH_C), lambda b, s, pt: (0, 0)),
                  pl.BlockSpec((1, LANES), lambda b, s, pt: (0, 0))]
        + 2 * _page_specs(layer, (PAGE * NH_C, LANES)),
        out_specs=new,
        scratch_shapes=[pltpu.VMEM((rows, 1), F32), pltpu.VMEM((rows, 1), F32), pltpu.VMEM((rows, LANES), F32)])
    return pl.pallas_call(
        functools.partial(_diff_sample_body, t_new=t_new, lambda_init=lambda_init), grid_spec=grid_spec,
        out_shape=jax.ShapeDtypeStruct(qn.shape, F32),
        compiler_params=_params(("parallel", "arbitrary")), name="diff_sample",
    )(page_table, qn, kn, vn, lam, gsub, *([cache_k] * PAGES_PER_STEP), *([cache_v] * PAGES_PER_STEP))


def _row(v):
    return v.reshape(1, -1).astype(F32)


def _even_weights(w_in, b_gate, g_q, g_k):
    w_a = w_in[:, :4 * W_A].astype(BF16)
    g0 = 4 * W_A
    w_g = jnp.pad(w_in[:, g0:g0 + 2 * NH_A], ((0, 0), (0, LANES - 2 * NH_A))).astype(BF16)
    b0 = g0 + 2 * NH_A
    w_q, w_k, w_v = (w_in[:, b0 + i * W_B:b0 + (i + 1) * W_B].astype(BF16) for i in range(3))
    bias = jnp.pad(_row(b_gate), ((0, 0), (0, LANES - 2 * NH_A)))
    zeros = jnp.zeros((1, W_B), F32)
    weights = [w_a, w_g, w_q, w_k, w_v]
    auxes = [jnp.zeros((1, 4 * W_A), F32), bias, _row(jnp.tile(g_q, NH_B)), _row(jnp.tile(g_k, NH_B)), zeros]
    return weights, auxes, ["none", "bias", "half", "half", "none"]


def _ffn_weights(w_gate, w_up, w_down):
    d_ff = w_gate.shape[1]
    n = d_ff // FF_CHUNK
    wg = w_gate.astype(BF16).reshape(D_MODEL, n, FF_CHUNK).transpose(1, 0, 2)
    wu = w_up.astype(BF16).reshape(D_MODEL, n, FF_CHUNK).transpose(1, 0, 2)
    wd = w_down.astype(BF16).reshape(n, FF_CHUNK, D_MODEL)
    return wg, wu, wd


def _mlstm_group(a, gates, c0, n0, m0, g_out, batch, seq):
    L = MLSTM_CHUNK
    live, n_chunks = (L, seq // L) if seq % L == 0 else (seq, 1)
    g_col, g_row = gate_prep(gates, live)
    ha, c1, n1, m1 = mlstm(a, g_col, g_row, c0.reshape(batch * NH_A, DH_A, DH_A),
                           n0.reshape(batch * NH_A, 1, DH_A),
                           jnp.broadcast_to(m0.reshape(batch * NH_A, 1, 1), (batch * NH_A, 1, LANES)),
                           g_out, batch, n_chunks, live)
    return (ha, c1.reshape(batch, NH_A, DH_A, DH_A), n1.reshape(batch, NH_A, DH_A),
            m1[:, 0, 0].reshape(batch, NH_A))


def kernel(x_prompt, x_sample, cache_moba_k, cache_moba_v, cache_diff_k, cache_diff_v, state_mlstm_c, state_mlstm_n, state_mlstm_m, page_table, norm_mix, norm_ffn, w_in_even, b_gate_even, g_out_mlstm, g_q_moba, g_k_moba, w_out_even, w_in_odd, g_q_diff, g_k_diff, lam_diff, g_sub_diff, w_out_odd, w_gate, w_up, w_down):
    bp, seq, _ = x_prompt.shape
    bs, dec, _ = x_sample.shape
    n_p, n_s = bp * seq, bs * dec
    tm_p, tm_s = 512, n_s
    depth = norm_mix.shape[0]
    xp = x_prompt.reshape(n_p, D_MODEL)
    xs = x_sample.reshape(n_s, D_MODEL)
    c_init = jnp.zeros((bp, NH_A, DH_A, DH_A), F32)
    n_init = jnp.zeros((bp, NH_A, DH_A), F32)
    m_init = jnp.full((bp, NH_A), NEG, F32)
    n_kblocks = seq // MOBA_BLOCK
    moba_k_pages = cache_moba_k.transpose(0, 1, 3, 4, 2)
    moba_v_pages = cache_moba_v.transpose(0, 1, 3, 4, 2)
    diff_k_pages = cache_diff_k.reshape(cache_diff_k.shape[:2] + (PAGE * NH_C, 2 * DH_C))
    diff_v_pages = cache_diff_v.reshape(cache_diff_v.shape[:2] + (PAGE * NH_C, 2 * DH_C))

    outs = {k: [] for k in ("p_mk", "p_mv", "p_dk", "p_dv", "p_c", "p_n", "p_m",
                            "s_mk", "s_mv", "s_dk", "s_dv", "s_c", "s_n", "s_m")}
    for layer in range(depth):
        gn_mix = _row(norm_mix[layer])
        gn_ffn = _row(norm_ffn[layer])
        wg, wu, wd = _ffn_weights(w_gate[layer], w_up[layer], w_down[layer])
        if layer % 2 == 0:
            e = layer // 2
            weights, auxes, kinds = _even_weights(w_in_even[e], b_gate_even[e], g_q_moba[e], g_k_moba[e])
            g_out = _row(g_out_mlstm[e])
            wo = w_out_even[e].astype(BF16)

            forms_p = [["f32"], ["f32"], ["q16"], ["f32", "b16", "mean"], ["f32", "vt"]]
            a, gates, q16, kb, k16, kmean, vb, vt = norm_proj(xp, gn_mix, weights, auxes, kinds, forms_p, tm_p)
            ha, c1, n1, m1 = _mlstm_group(a, gates, c_init, n_init, m_init, g_out, bp, seq)
            hb = flash_prompt(q16, k16, vt, [kmean.reshape(bp, n_kblocks, W_B)], "moba", bp, seq)
            xp = mix_ffn(xp, [ha, hb], wo, gn_ffn, wg, wu, wd, tm_p)
            outs["p_mk"].append(kb.reshape(bp, seq, NH_B, DH_B))
            outs["p_mv"].append(vb.reshape(bp, seq, NH_B, DH_B))
            outs["p_c"].append(c1); outs["p_n"].append(n1); outs["p_m"].append(m1)

            a, gates, qb, kb, vb = norm_proj(xs, gn_mix, weights, auxes, kinds, [["f32"]] * 5, tm_s)
            ha, c1, n1, m1 = _mlstm_group(a, gates, state_mlstm_c[e], state_mlstm_n[e], state_mlstm_m[e],
                                          g_out, bs, dec)
            hb = moba_sample(page_table, qb, kb, vb, moba_k_pages, moba_v_pages, e)
            xs = mix_ffn(xs, [ha, hb], wo, gn_ffn, wg, wu, wd, tm_s)
            outs["s_mk"].append(kb.reshape(bs, dec, NH_B, DH_B))
            outs["s_mv"].append(vb.reshape(bs, dec, NH_B, DH_B))
            outs["s_c"].append(c1); outs["s_n"].append(n1); outs["s_m"].append(m1)
        else:
            o = layer // 2
            lambda_init = 0.8 - 0.6 * math.exp(-0.3 * layer)
            w_in = w_in_odd[o].astype(BF16)
            weights = [w_in[:, i * D_MODEL:(i + 1) * D_MODEL] for i in range(3)]
            auxes = [_row(jnp.tile(g_q_diff[o].reshape(-1), NH_C)), _row(jnp.tile(g_k_diff[o].reshape(-1), NH_C)),
                     jnp.zeros((1, D_MODEL), F32)]
            kinds = ["half", "half", "none"]
            lam = lam_diff[o].astype(F32)
            gsub = _row(g_sub_diff[o])
            wo = w_out_odd[o].astype(BF16)

            forms_p = [["q16"], ["f32", "b16"], ["f32", "vt"]]
            q16, k, k16, v, vt = norm_proj(xp, gn_mix, weights, auxes, kinds, forms_p, tm_p)
            att = flash_prompt(q16, k16, vt, [lam, gsub], "diff", bp, seq, lambda_init)
            xp = mix_ffn(xp, [att], wo, gn_ffn, wg, wu, wd, tm_p)
            outs["p_dk"].append(k.reshape(bp, seq, NH_C, 2 * DH_C))
            outs["p_dv"].append(v.reshape(bp, seq, NH_C, 2 * DH_C))

            q, k, v = norm_proj(xs, gn_mix, weights, auxes, kinds, [["f32"]] * 3, tm_s)
            att = diff_sample(page_table, q, k, v, lam, gsub, diff_k_pages, diff_v_pages, o, lambda_init)
            xs = mix_ffn(xs, [att], wo, gn_ffn, wg, wu, wd, tm_s)
            outs["s_dk"].append(k.reshape(bs, dec, NH_C, 2 * DH_C))
            outs["s_dv"].append(v.reshape(bs, dec, NH_C, 2 * DH_C))

    st = {k: jnp.stack(v) for k, v in outs.items()}
    return (xp.reshape(bp, seq, D_MODEL), xs.reshape(bs, dec, D_MODEL),
            st["p_mk"], st["p_mv"], st["p_dk"], st["p_dv"], st["p_c"], st["p_n"], st["p_m"],
            st["s_mk"], st["s_mv"], st["s_dk"], st["s_dv"], st["s_c"], st["s_n"], st["s_m"])
```

```python
import functools
import math

import jax
import jax.numpy as jnp
from jax import lax
from jax.experimental import pallas as pl
from jax.experimental.pallas import tpu as pltpu

F32 = jnp.float32
BF16 = jnp.bfloat16

D_MODEL = 1024
NH_A, DH_A = 4, 128
W_A = NH_A * DH_A
NH_B, DH_B = 8, 64
W_B = NH_B * DH_B
NH_C, DH_C = 8, 64
MLSTM_CHUNK = 128
MOBA_BLOCK = 256
MOBA_TOPK = 3
PAGE = 128
EPS = 1e-6
NEG = -1e30
LANES = 128
FF_CHUNK = 256
FLASH_Q_TILE = 512
VMEM_LIMIT = 56 * 1024 * 1024


def _dot(a, b):
    return jnp.dot(a, b, preferred_element_type=F32)


def _dot_nt(a, b):
    return lax.dot_general(a, b, (((1,), (1,)), ((), ())), preferred_element_type=F32)


def _rms(x, g):
    ms = jnp.mean(x * x, axis=-1, keepdims=True)
    return x * lax.rsqrt(ms + EPS) * g


def _half_norm(y, g):
    outs = []
    for c in range(y.shape[1] // LANES):
        yc = y[:, c * LANES:(c + 1) * LANES]
        sq = yc * yc
        lo = lax.broadcasted_iota(jnp.int32, yc.shape, 1) < 64
        s_lo = jnp.sum(jnp.where(lo, sq, 0.0), axis=-1, keepdims=True)
        s_hi = jnp.sum(jnp.where(lo, 0.0, sq), axis=-1, keepdims=True)
        ms = jnp.where(lo, s_lo, s_hi) * (1.0 / 64)
        outs.append(yc * lax.rsqrt(ms + EPS))
    return jnp.concatenate(outs, axis=-1) * g


def _params(sem):
    return pltpu.CompilerParams(dimension_semantics=sem, vmem_limit_bytes=VMEM_LIMIT)


def _resident(arr):
    nd = arr.ndim
    return pl.BlockSpec(arr.shape, lambda i: (0,) * nd, pipeline_mode=pl.Buffered(1))


def _form_shape(form, n_rows, width, tm):
    nb, tb = n_rows // MOBA_BLOCK, max(tm // MOBA_BLOCK, 1)
    if form == "f32":
        return (n_rows, width), F32, (tm, width), lambda i: (i, 0)
    if form in ("b16", "q16"):
        return (n_rows, width), BF16, (tm, width), lambda i: (i, 0)
    if form == "vt":
        return (nb, width // LANES, LANES, MOBA_BLOCK), BF16, (tb, width // LANES, LANES, MOBA_BLOCK), lambda i: (i, 0, 0, 0)
    assert form == "mean"
    return (nb, 1, width), F32, (tb, 1, width), lambda i: (i, 0, 0)


def _norm_proj_body(x_ref, gn_ref, *refs, kinds, forms):
    n = len(kinds)
    w_refs, aux_refs, out_refs = refs[:n], refs[n:2 * n], list(refs[2 * n:])
    h = _rms(x_ref[...], gn_ref[...]).astype(BF16)
    for idx, kind in enumerate(kinds):
        w_ref, aux_ref = w_refs[idx], aux_refs[idx]
        outs = {form: out_refs.pop(0) for form in forms[idx]}
        width = w_ref.shape[1]
        step = min(width, 512)
        for c0 in range(0, width, step):
            y = _dot(h, w_ref[:, c0:c0 + step])
            if kind == "bias":
                y = y + aux_ref[:, c0:c0 + step]
            elif kind == "half":
                y = _half_norm(y, aux_ref[:, c0:c0 + step])
            if "f32" in outs:
                outs["f32"][:, c0:c0 + step] = y
            if "b16" in outs:
                outs["b16"][:, c0:c0 + step] = y.astype(BF16)
            if "q16" in outs:
                outs["q16"][:, c0:c0 + step] = (y * (DH_B ** -0.5)).astype(BF16)
            for r in range(y.shape[0] // MOBA_BLOCK):
                blk = y[r * MOBA_BLOCK:(r + 1) * MOBA_BLOCK, :]
                if "mean" in outs:
                    outs["mean"][r, :, c0:c0 + step] = jnp.mean(blk, axis=0, keepdims=True)
                if "vt" in outs:
                    for g in range(step // LANES):
                        tile = jnp.transpose(blk[:, g * LANES:(g + 1) * LANES])
                        outs["vt"][r, c0 // LANES + g] = tile.astype(BF16)


def norm_proj(x, gn, weights, auxes, kinds, forms, tm):
    n_rows = x.shape[0]
    in_specs = [pl.BlockSpec((tm, D_MODEL), lambda i: (i, 0)), _resident(gn)]
    in_specs += [_resident(w) for w in weights] + [_resident(a) for a in auxes]
    out_shape, out_specs = [], []
    for w, fs in zip(weights, forms):
        for form in fs:
            shape, dtype, block, imap = _form_shape(form, n_rows, w.shape[1], tm)
            out_shape.append(jax.ShapeDtypeStruct(shape, dtype))
            out_specs.append(pl.BlockSpec(block, imap))
    return pl.pallas_call(
        functools.partial(_norm_proj_body, kinds=tuple(kinds), forms=tuple(tuple(f) for f in forms)),
        grid=(n_rows // tm,), in_specs=in_specs, out_specs=out_specs, out_shape=out_shape,
        compiler_params=_params(("parallel",)), name="norm_proj",
    )(x, gn, *weights, *auxes)


def _mix_ffn_body(x_ref, *refs, n_mix):
    a_refs = refs[:n_mix]
    wo_ref, gn_ref, wg_ref, wu_ref, wd_ref, out_ref = refs[n_mix:]
    mixed = jnp.concatenate([a_ref[...] for a_ref in a_refs], axis=-1).astype(BF16)
    x1 = x_ref[...] + _dot(mixed, wo_ref[...])
    h = _rms(x1, gn_ref[...]).astype(BF16)
    acc = x1
    for c in range(wg_ref.shape[0]):
        g = _dot(h, wg_ref[c])
        u = _dot(h, wu_ref[c])
        act = (g * jax.nn.sigmoid(g) * u).astype(BF16)
        acc = acc + _dot(act, wd_ref[c])
    out_ref[...] = acc


def mix_ffn(x, mixes, wo, gn, wg, wu, wd, tm):
    n_rows = x.shape[0]
    in_specs = [pl.BlockSpec((tm, D_MODEL), lambda i: (i, 0))]
    in_specs += [pl.BlockSpec((tm, a.shape[1]), lambda i: (i, 0)) for a in mixes]
    in_specs += [_resident(w) for w in (wo, gn, wg, wu, wd)]
    return pl.pallas_call(
        functools.partial(_mix_ffn_body, n_mix=len(mixes)),
        grid=(n_rows // tm,), in_specs=in_specs,
        out_specs=pl.BlockSpec((tm, D_MODEL), lambda i: (i, 0)),
        out_shape=jax.ShapeDtypeStruct((n_rows, D_MODEL), F32),
        compiler_params=_params(("parallel",)), name="mix_ffn",
    )(x, *mixes, wo, gn, wg, wu, wd)


def _log_sigmoid(x):
    return jnp.minimum(x, 0.0) - jnp.log1p(jnp.exp(-jnp.abs(x)))


def _pad_rows(x, rows):
    if x.shape[0] == rows:
        return x
    return jnp.concatenate([x, jnp.zeros((rows - x.shape[0],) + x.shape[1:], x.dtype)], axis=0)


def _gate_prep_body(g_ref, col_ref, row_ref, *, valid_len):
    chunk = MLSTM_CHUNK
    for c in range(col_ref.shape[0] // chunk):
        g = _pad_rows(g_ref[c * valid_len:(c + 1) * valid_len, :], chunk)
        lane = lax.broadcasted_iota(jnp.int32, g.shape, 1)
        row = lax.broadcasted_iota(jnp.int32, g.shape, 0)
        is_f = (lane >= NH_A) & (lane < 2 * NH_A)
        live = row < valid_len
        logf = jnp.where(is_f & live, _log_sigmoid(g), 0.0)
        k = 1
        while k < chunk:
            logf = logf + jnp.where(row >= k, pltpu.roll(logf, k, 0), 0.0)
            k *= 2
        out = jnp.where(is_f, logf, jnp.where(live, g, NEG))
        col_ref[c * chunk:(c + 1) * chunk, :] = out
        row_ref[:, c * chunk:(c + 1) * chunk] = jnp.transpose(out)[:8, :]


def gate_prep(gates, valid_len):
    chunk = MLSTM_CHUNK
    n_chunks = gates.shape[0] // valid_len
    per_tile = 4 if (valid_len == chunk and n_chunks % 4 == 0) else 1
    return pl.pallas_call(
        functools.partial(_gate_prep_body, valid_len=valid_len),
        grid=(n_chunks // per_tile,),
        in_specs=[pl.BlockSpec((per_tile * valid_len, LANES), lambda i: (i, 0))],
        out_specs=[pl.BlockSpec((per_tile * chunk, LANES), lambda i: (i, 0)),
                   pl.BlockSpec((8, per_tile * chunk), lambda i: (0, i))],
        out_shape=[jax.ShapeDtypeStruct((n_chunks * chunk, LANES), F32),
                   jax.ShapeDtypeStruct((8, n_chunks * chunk), F32)],
        compiler_params=_params(("parallel",)), name="gate_prep",
    )(gates)


def _mlstm_body(q_ref, k_ref, v_ref, o_ref, gc_ref, gr_ref, c0_ref, n0_ref, m0_ref, gout_ref,
                h_ref, c_out, n_out, m_out, c_s, n_s, m_s):
    chunk_id = pl.program_id(1)
    L = MLSTM_CHUNK
    live = q_ref.shape[0]

    @pl.when(chunk_id == 0)
    def _():
        c_s[...] = c0_ref[...]
        n_s[...] = n0_ref[...]
        m_s[...] = m0_ref[...]

    gc = gc_ref[...]
    gr = gr_ref[...]
    lane = lax.broadcasted_iota(jnp.int32, gc.shape, 1)
    sub = lax.broadcasted_iota(jnp.int32, gr.shape, 0)
    tt = lax.broadcasted_iota(jnp.int32, (L, L), 0)
    ss = lax.broadcasted_iota(jnp.int32, (L, L), 1)
    for head in range(NH_A):
        cols = slice(head * DH_A, (head + 1) * DH_A)
        q = _pad_rows(q_ref[:, cols], L)
        k = _pad_rows(k_ref[:, cols], L) * (DH_A ** -0.5)
        v = _pad_rows(v_ref[:, cols], L)
        i_col = jnp.sum(jnp.where(lane == head, gc, 0.0), axis=-1, keepdims=True)
        b_col = jnp.sum(jnp.where(lane == head + NH_A, gc, 0.0), axis=-1, keepdims=True)
        i_row = jnp.sum(jnp.where(sub == head, gr, 0.0), axis=0, keepdims=True)
        b_row = jnp.sum(jnp.where(sub == head + NH_A, gr, 0.0), axis=0, keepdims=True)
        m_prev = m_s[head][:, 0:1]
        b_last = b_col[L - 1:L, :]
        c_prev = c_s[head]
        n_prev = n_s[head]

        log_d = jnp.where(ss <= tt, b_col - b_row + i_row, NEG)
        inter = b_col + m_prev
        m_t = jnp.maximum(inter, jnp.max(log_d, axis=-1, keepdims=True))
        w_intra = jnp.exp(log_d - m_t)
        w_inter = jnp.exp(inter - m_t)
        qb, kb, vb = q.astype(BF16), k.astype(BF16), v.astype(BF16)
        s = _dot_nt(qb, kb) * w_intra
        num = w_inter * _dot_nt(qb, c_prev.astype(BF16)) + _dot(s.astype(BF16), vb)
        den = w_inter * jnp.sum(q * n_prev, axis=-1, keepdims=True) + jnp.sum(s, axis=-1, keepdims=True)
        hh = num / jnp.maximum(jnp.abs(den), jnp.exp(-m_t))
        m_new = m_t[L - 1:L, :]
        w_c = jnp.exp(b_last + m_prev - m_new)
        w_s = jnp.exp(b_last - b_col + i_col - m_new)
        vw_t = jnp.transpose(v * w_s).astype(BF16)
        c_new = w_c * c_prev + _dot(vw_t, kb)
        n_new = w_c * n_prev + jnp.sum(w_s * k, axis=0, keepdims=True)
        m_row = jnp.broadcast_to(m_new, (1, LANES))
        c_s[head] = c_new
        n_s[head] = n_new
        m_s[head] = m_row

        hn = hh * lax.rsqrt(jnp.mean(hh * hh, axis=-1, keepdims=True) + EPS) * gout_ref[...]
        h_ref[:, cols] = hn[:live] * jax.nn.sigmoid(o_ref[:, cols])
        c_out[head] = c_new
        n_out[head] = n_new
        m_out[head] = m_row


def mlstm(a, g_col, g_row, c0, n0, m0, g_out, batch, n_chunks, live):
    L = MLSTM_CHUNK
    n_rows = a.shape[0]

    def part(p):
        return pl.BlockSpec((live, W_A), lambda b, c: (b * n_chunks + c, p))

    def state(shape):
        return pl.BlockSpec((NH_A,) + shape, lambda b, c: (b, 0, 0))

    in_specs = [part(0), part(1), part(2), part(3),
                pl.BlockSpec((L, LANES), lambda b, c: (b * n_chunks + c, 0)),
                pl.BlockSpec((8, L), lambda b, c: (0, b * n_chunks + c)),
                state((DH_A, DH_A)), state((1, DH_A)), state((1, LANES)),
                pl.BlockSpec((1, DH_A), lambda b, c: (0, 0))]
    out_specs = [part(0), state((DH_A, DH_A)), state((1, DH_A)), state((1, LANES))]
    out_shape = [jax.ShapeDtypeStruct((n_rows, W_A), F32),
                 jax.ShapeDtypeStruct((batch * NH_A, DH_A, DH_A), F32),
                 jax.ShapeDtypeStruct((batch * NH_A, 1, DH_A), F32),
                 jax.ShapeDtypeStruct((batch * NH_A, 1, LANES), F32)]
    return pl.pallas_call(
        _mlstm_body, grid=(batch, n_chunks), in_specs=in_specs, out_specs=out_specs,
        out_shape=out_shape,
        scratch_shapes=[pltpu.VMEM((NH_A, DH_A, DH_A), F32), pltpu.VMEM((NH_A, 1, DH_A), F32),
                        pltpu.VMEM((NH_A, 1, LANES), F32)],
        compiler_params=_params(("parallel", "arbitrary")), name="mlstm",
    )(a, a, a, a, g_col, g_row, c0, n0, m0, g_out)


def _topk_bias(gate, valid):
    lane = lax.broadcasted_iota(jnp.int32, gate.shape, 1)
    g = jnp.where(valid, gate, NEG)
    chosen = jnp.zeros(gate.shape, jnp.bool_)
    for _ in range(MOBA_TOPK):
        top = jnp.max(g, axis=-1, keepdims=True)
        first = jnp.min(jnp.where(g == top, lane, LANES), axis=-1, keepdims=True)
        pick = lane == first
        chosen = chosen | pick
        g = jnp.where(pick, NEG, g)
    return jnp.where(chosen & valid, 0.0, NEG)


def _lambda(lam_ref, lambda_init):
    lp = lam_ref[...]
    s01 = jnp.sum(lp[0:1, :] * lp[1:2, :], axis=-1, keepdims=True)
    s23 = jnp.sum(lp[2:3, :] * lp[3:4, :], axis=-1, keepdims=True)
    return jnp.exp(s01) - jnp.exp(s23) + lambda_init


def _topk_bias_rows(gate, valid):
    row = lax.broadcasted_iota(jnp.int32, gate.shape, 0)
    g = jnp.where(valid, gate, NEG)
    chosen = jnp.zeros(gate.shape, jnp.bool_)
    for _ in range(MOBA_TOPK):
        top = jnp.max(g, axis=0, keepdims=True)
        first = jnp.min(jnp.where(g == top, row, gate.shape[0]), axis=0, keepdims=True)
        pick = row == first
        chosen = chosen | pick
        g = jnp.where(pick, NEG, g)
    return jnp.where(chosen & valid, 0.0, NEG)


def _flash_body(*refs, mode, tq, tk, lambda_init):
    if mode == "moba":
        q_ref, k_ref, vt_ref, km_ref, o_ref, m_s, l_s, acc_s, s_buf, sel_s = refs
    else:
        q_ref, k_ref, vt_ref, lam_ref, gsub_ref, o_ref, m_s, l_s, acc_s, s_buf = refs
    per = tq // tk
    n_past = pl.program_id(2) * per
    lo = lax.broadcasted_iota(jnp.int32, (tq, LANES), 1) < 64
    q = q_ref[...]
    zero = jnp.zeros_like(q)
    q2 = jnp.concatenate([jnp.where(lo, q, zero), jnp.where(lo, zero, q)], axis=0)
    col = lax.broadcasted_iota(jnp.int32, (1, 2 * tq), 1)
    t_in_tile = jnp.where(col >= tq, col - tq, col)

    def scores(j):
        start = pl.multiple_of(j * tk, tk)
        return _dot_nt(k_ref[pl.ds(start, tk), :], q2)

    def absorb(j, sj):
        m_old = m_s[...]
        m_new = jnp.maximum(m_old, jnp.max(sj, axis=0, keepdims=True))
        alpha = jnp.exp(m_old - m_new)
        pj = jnp.exp(sj - m_new)
        l_s[...] = alpha * l_s[...] + jnp.sum(pj, axis=0, keepdims=True)
        acc_s[...] = alpha * acc_s[...] + _dot(vt_ref[j, 0], pj.astype(BF16))
        m_s[...] = m_new

    m_s[...] = jnp.full(m_s.shape, NEG, F32)
    l_s[...] = jnp.zeros(l_s.shape, F32)
    acc_s[...] = jnp.zeros(acc_s.shape, F32)
    if mode == "moba":
        gate = _dot_nt(km_ref[0].astype(BF16), q2)
        row = lax.broadcasted_iota(jnp.int32, gate.shape, 0)
        own = n_past + t_in_tile // tk
        sel_s[...] = jnp.where(row == own, 0.0, _topk_bias_rows(gate, row < own))

    s_buf[...] = scores(0)

    def body(j, carry):
        sj = s_buf[...]
        s_buf[...] = scores(j + 1)
        if mode == "moba":
            sj = sj + sel_s[pl.ds(j, 1), :]
        absorb(j, sj)
        return carry

    lax.fori_loop(0, n_past, body, 0)

    key = lax.broadcasted_iota(jnp.int32, (tk, 2 * tq), 0)
    for d in range(per):
        sd = s_buf[...]
        if d + 1 < per:
            s_buf[...] = scores(n_past + d + 1)
        if mode == "moba":
            sd = sd + sel_s[pl.ds(n_past + d, 1), :]
        absorb(n_past + d, jnp.where(key + d * tk <= t_in_tile, sd, NEG))

    o2t = acc_s[...] / l_s[...]
    c1 = jnp.transpose(o2t[:, :tq])
    c2 = jnp.transpose(o2t[:, tq:])
    if mode == "moba":
        o_ref[...] = jnp.where(lo, c1, c2)
    else:
        o = c1 - _lambda(lam_ref, lambda_init) * c2
        on = o * lax.rsqrt(jnp.mean(o * o, axis=-1, keepdims=True) + EPS) * gsub_ref[...]
        o_ref[...] = on * (1.0 - lambda_init)


def flash_prompt(q, k, vt, extras, mode, batch, seq, lambda_init=0.0):
    tk = MOBA_BLOCK
    tq = FLASH_Q_TILE
    groups = q.shape[1] // LANES
    nq, nk = seq // tq, seq // tk
    in_specs = [pl.BlockSpec((tq, LANES), lambda b, g, i: (b * nq + i, g)),
                pl.BlockSpec((seq, LANES), lambda b, g, i: (b, g)),
                pl.BlockSpec((nk, 1, LANES, tk), lambda b, g, i: (b, g, 0, 0))]
    scratch = [pltpu.VMEM((1, 2 * tq), F32), pltpu.VMEM((1, 2 * tq), F32), pltpu.VMEM((LANES, 2 * tq), F32),
               pltpu.VMEM((tk, 2 * tq), F32)]
    if mode == "moba":
        in_specs.append(pl.BlockSpec((1, nk, LANES), lambda b, g, i: (b, 0, g)))
        scratch.append(pltpu.VMEM((nk, 2 * tq), F32))
    else:
        in_specs += [pl.BlockSpec((4, DH_C), lambda b, g, i: (0, 0)), pl.BlockSpec((1, LANES), lambda b, g, i: (0, 0))]
    return pl.pallas_call(
        functools.partial(_flash_body, mode=mode, tq=tq, tk=tk, lambda_init=lambda_init),
        grid=(batch, groups, nq), in_specs=in_specs,
        out_specs=pl.BlockSpec((tq, LANES), lambda b, g, i: (b * nq + i, g)),
        out_shape=jax.ShapeDtypeStruct(q.shape, F32), scratch_shapes=scratch,
        compiler_params=_params(("parallel", "parallel", "arbitrary")), name="flash_" + mode,
    )(q, k, vt, *extras)


PAGES_PER_STEP = 8


def _page_specs(layer, page_shape):
    def spec(r):
        return pl.BlockSpec((1, 1) + page_shape,
                            lambda b, s, pt: (layer, pt[b, PAGES_PER_STEP * s + r]) + (0,) * len(page_shape))
    return [spec(r) for r in range(PAGES_PER_STEP)]


def _moba_sample_body(pt_ref, q_ref, kn_ref, vn_ref, *refs, t_new):
    k_refs, v_refs = refs[:PAGES_PER_STEP], refs[PAGES_PER_STEP:2 * PAGES_PER_STEP]
    o_ref, part_s, m_s, l_s, g_s = refs[2 * PAGES_PER_STEP:]
    step = pl.program_id(1)
    n_steps = pl.num_programs(1)
    rows = NH_B * t_new
    pages_per_block = MOBA_BLOCK // PAGE
    blocks_per_step = PAGES_PER_STEP // pages_per_block
    lane = lax.broadcasted_iota(jnp.int32, (rows, LANES), 1)

    @pl.when(step == 0)
    def _():
        m_s[...] = jnp.full(m_s.shape, NEG, F32)
        l_s[...] = jnp.zeros(l_s.shape, F32)
        g_s[...] = jnp.full(g_s.shape, NEG, F32)

    qs = q_ref[...] * (DH_B ** -0.5)
    q_heads = [qs[:, h * DH_B:(h + 1) * DH_B].astype(BF16) for h in range(NH_B)]

    def block_partial(score, weighted_values, mask):
        s = jnp.concatenate([score(h) for h in range(NH_B)], axis=0)
        gate = jnp.sum(s, axis=-1, keepdims=True)
        if mask is not None:
            s = jnp.where(mask, s, NEG)
        m = jnp.max(s, axis=-1, keepdims=True)
        p = jnp.exp(s - m)
        l = jnp.sum(p, axis=-1, keepdims=True)
        o = jnp.concatenate([weighted_values(h, p[h * t_new:(h + 1) * t_new].astype(BF16))
                             for h in range(NH_B)], axis=0)
        return m, l, gate, o

    for r in range(blocks_per_step):
        pages = range(r * pages_per_block, (r + 1) * pages_per_block)

        def score(h, pages=pages):
            kt = jnp.concatenate([k_refs[p][0, 0, h] for p in pages], axis=1).astype(BF16)
            return _dot(q_heads[h], kt)

        def weighted_values(h, ph, pages=pages):
            vt = jnp.concatenate([v_refs[p][0, 0, h] for p in pages], axis=1).astype(BF16)
            return _dot_nt(ph, vt)

        m, l, gate, o = block_partial(score, weighted_values, None)
        j = step * blocks_per_step + r
        part_s[j] = o
        here = lane == j
        m_s[...] = jnp.where(here, m, m_s[...])
        l_s[...] = jnp.where(here, l, l_s[...])
        g_s[...] = jnp.where(here, gate, g_s[...])

    @pl.when(step == n_steps - 1)
    def _():
        n_past = n_steps * blocks_per_step
        pad = jnp.zeros((LANES - t_new, W_B), F32)
        kn = jnp.concatenate([kn_ref[...], pad], axis=0)
        vn = jnp.concatenate([vn_ref[...], pad], axis=0)
        rr = lax.broadcasted_iota(jnp.int32, (rows, LANES), 0)
        causal = lane <= rr % t_new
        m_own, l_own, _, o_own = block_partial(
            lambda h: _dot_nt(q_heads[h], kn[:, h * DH_B:(h + 1) * DH_B].astype(BF16)),
            lambda h, ph: _dot(ph, vn[:, h * DH_B:(h + 1) * DH_B].astype(BF16)), causal)
        bias = _topk_bias(g_s[...], lane < n_past)
        m_all = jnp.maximum(jnp.max(m_s[...] + bias, axis=-1, keepdims=True), m_own)
        w = jnp.exp(m_s[...] + bias - m_all)
        w_own = jnp.exp(m_own - m_all)
        den = jnp.sum(w * l_s[...], axis=-1, keepdims=True) + w_own * l_own
        acc = w_own * o_own
        for j in range(n_past):
            acc = acc + w[:, j:j + 1] * part_s[j]
        out = acc / den
        o_ref[...] = jnp.concatenate([out[h * t_new:(h + 1) * t_new] for h in range(NH_B)], axis=-1)


def moba_sample(page_table, qn, kn, vn, cache_k, cache_v, layer):
    batch, n_pages = page_table.shape
    t_new = qn.shape[0] // batch
    n_steps = n_pages // PAGES_PER_STEP
    n_blocks = n_pages * PAGE // MOBA_BLOCK
    rows = NH_B * t_new
    new = pl.BlockSpec((t_new, W_B), lambda b, s, pt: (b, 0))
    grid_spec = pltpu.PrefetchScalarGridSpec(
        num_scalar_prefetch=1, grid=(batch, n_steps),
        in_specs=[new, new, new] + 2 * _page_specs(layer, (NH_B, DH_B, PAGE)),
        out_specs=new,
        scratch_shapes=[pltpu.VMEM((n_blocks, rows, DH_B), F32), pltpu.VMEM((rows, LANES), F32),
                        pltpu.VMEM((rows, LANES), F32), pltpu.VMEM((rows, LANES), F32)])
    return pl.pallas_call(
        functools.partial(_moba_sample_body, t_new=t_new), grid_spec=grid_spec,
        out_shape=jax.ShapeDtypeStruct(qn.shape, F32),
        compiler_params=_params(("parallel", "arbitrary")), name="moba_sample",
    )(page_table, qn, kn, vn, *([cache_k] * PAGES_PER_STEP), *([cache_v] * PAGES_PER_STEP))


def _diff_sample_body(pt_ref, q_ref, kn_ref, vn_ref, lam_ref, gsub_ref, *refs, t_new, lambda_init):
    k_refs, v_refs = refs[:PAGES_PER_STEP], refs[PAGES_PER_STEP:2 * PAGES_PER_STEP]
    o_ref, m_s, l_s, acc_s = refs[2 * PAGES_PER_STEP:]
    step = pl.program_id(1)
    n_steps = pl.num_programs(1)
    per_head = 2 * t_new
    rows = NH_C * per_head

    @pl.when(step == 0)
    def _():
        m_s[...] = jnp.full(m_s.shape, NEG, F32)
        l_s[...] = jnp.zeros(l_s.shape, F32)
        acc_s[...] = jnp.zeros(acc_s.shape, F32)

    qs = q_ref[...] * (DH_C ** -0.5)
    lo = lax.broadcasted_iota(jnp.int32, (t_new, LANES), 1) < DH_C
    q_heads = []
    for h in range(NH_C):
        qh = qs[:, h * LANES:(h + 1) * LANES]
        q_heads.append(jnp.concatenate([jnp.where(lo, qh, 0.0), jnp.where(lo, 0.0, qh)], axis=0).astype(BF16))

    def update(k_of_head, v_of_head, mask):
        s = jnp.concatenate([_dot_nt(q_heads[h], k_of_head(h)) for h in range(NH_C)], axis=0)
        if mask is not None:
            s = jnp.where(mask, s, NEG)
        m_old = m_s[...]
        m_new = jnp.maximum(m_old, jnp.max(s, axis=-1, keepdims=True))
        alpha = jnp.exp(m_old - m_new)
        p = jnp.exp(s - m_new)
        l_s[...] = alpha * l_s[...] + jnp.sum(p, axis=-1, keepdims=True)
        pv = jnp.concatenate([_dot(p[h * per_head:(h + 1) * per_head].astype(BF16), v_of_head(h))
                              for h in range(NH_C)], axis=0)
        acc_s[...] = alpha * acc_s[...] + pv
        m_s[...] = m_new

    def head_rows(page_refs, h):
        rows_h = [r[0, 0, pl.ds(h, PAGE, stride=NH_C), :] for r in page_refs]
        return jnp.concatenate(rows_h, axis=0).astype(BF16)

    update(lambda h: head_rows(k_refs, h), lambda h: head_rows(v_refs, h), None)

    @pl.when(step == n_steps - 1)
    def _():
        pad = jnp.zeros((LANES - t_new, NH_C * LANES), F32)
        kn = jnp.concatenate([kn_ref[...], pad], axis=0)
        vn = jnp.concatenate([vn_ref[...], pad], axis=0)
        rr = lax.broadcasted_iota(jnp.int32, (rows, LANES), 0)
        cc = lax.broadcasted_iota(jnp.int32, (rows, LANES), 1)
        update(lambda h: kn[:, h * LANES:(h + 1) * LANES].astype(BF16),
               lambda h: vn[:, h * LANES:(h + 1) * LANES].astype(BF16), cc <= rr % t_new)
        o2 = acc_s[...] / l_s[...]
        lam = _lambda(lam_ref, lambda_init)
        outs = []
        for h in range(NH_C):
            base = h * per_head
            o = o2[base:base + t_new] - lam * o2[base + t_new:base + per_head]
            on = o * lax.rsqrt(jnp.mean(o * o, axis=-1, keepdims=True) + EPS) * gsub_ref[...]
            outs.append(on * (1.0 - lambda_init))
        o_ref[...] = jnp.concatenate(outs, axis=-1)


def diff_sample(page_table, qn, kn, vn, lam, gsub, cache_k, cache_v, layer, lambda_init):
    batch, n_pages = page_table.shape
    t_new = qn.shape[0] // batch
    n_steps = n_pages // PAGES_PER_STEP
    rows = NH_C * 2 * t_new
    new = pl.BlockSpec((t_new, NH_C * LANES), lambda b, s, pt: (b, 0))
    grid_spec = pltpu.PrefetchScalarGridSpec(
        num_scalar_prefetch=1, grid=(batch, n_steps),
        in_specs=[new, new, new,
                  pl.BlockSpec((4, DH_C), lambda b, s, pt: (0, 0)),
                  pl.BlockSpec((1, LANES), lambda b, s, pt: (0, 0))]
        + 2 * _page_specs(layer, (PAGE * NH_C, LANES)),
        out_specs=new,
        scratch_shapes=[pltpu.VMEM((rows, 1), F32), pltpu.VMEM((rows, 1), F32), pltpu.VMEM((rows, LANES), F32)])
    return pl.pallas_call(
        functools.partial(_diff_sample_body, t_new=t_new, lambda_init=lambda_init), grid_spec=grid_spec,
        out_shape=jax.ShapeDtypeStruct(qn.shape, F32),
        compiler_params=_params(("parallel", "arbitrary")), name="diff_sample",
    )(page_table, qn, kn, vn, lam, gsub, *([cache_k] * PAGES_PER_STEP), *([cache_v] * PAGES_PER_STEP))


def _row(v):
    return v.reshape(1, -1).astype(F32)


def _even_weights(w_in, b_gate, g_q, g_k):
    w_a = w_in[:, :4 * W_A].astype(BF16)
    g0 = 4 * W_A
    w_g = jnp.pad(w_in[:, g0:g0 + 2 * NH_A], ((0, 0), (0, LANES - 2 * NH_A))).astype(BF16)
    b0 = g0 + 2 * NH_A
    w_q, w_k, w_v = (w_in[:, b0 + i * W_B:b0 + (i + 1) * W_B].astype(BF16) for i in range(3))
    bias = jnp.pad(_row(b_gate), ((0, 0), (0, LANES - 2 * NH_A)))
    zeros = jnp.zeros((1, W_B), F32)
    weights = [w_a, w_g, w_q, w_k, w_v]
    auxes = [jnp.zeros((1, 4 * W_A), F32), bias, _row(jnp.tile(g_q, NH_B)), _row(jnp.tile(g_k, NH_B)), zeros]
    return weights, auxes, ["none", "bias", "half", "half", "none"]


def _ffn_weights(w_gate, w_up, w_down):
    d_ff = w_gate.shape[1]
    n = d_ff // FF_CHUNK
    wg = w_gate.astype(BF16).reshape(D_MODEL, n, FF_CHUNK).transpose(1, 0, 2)
    wu = w_up.astype(BF16).reshape(D_MODEL, n, FF_CHUNK).transpose(1, 0, 2)
    wd = w_down.astype(BF16).reshape(n, FF_CHUNK, D_MODEL)
    return wg, wu, wd


def _mlstm_group(a, gates, c0, n0, m0, g_out, batch, seq):
    L = MLSTM_CHUNK
    live, n_chunks = (L, seq // L) if seq % L == 0 else (seq, 1)
    g_col, g_row = gate_prep(gates, live)
    ha, c1, n1, m1 = mlstm(a, g_col, g_row, c0.reshape(batch * NH_A, DH_A, DH_A),
                           n0.reshape(batch * NH_A, 1, DH_A),
                           jnp.broadcast_to(m0.reshape(batch * NH_A, 1, 1), (batch * NH_A, 1, LANES)),
                           g_out, batch, n_chunks, live)
    return (ha, c1.reshape(batch, NH_A, DH_A, DH_A), n1.reshape(batch, NH_A, DH_A),
            m1[:, 0, 0].reshape(batch, NH_A))


def kernel(x_prompt, x_sample, cache_moba_k, cache_moba_v, cache_diff_k, cache_diff_v, state_mlstm_c, state_mlstm_n, state_mlstm_m, page_table, norm_mix, norm_ffn, w_in_even, b_gate_even, g_out_mlstm, g_q_moba, g_k_moba, w_out_even, w_in_odd, g_q_diff, g_k_diff, lam_diff, g_sub_diff, w_out_odd, w_gate, w_up, w_down):
    bp, seq, _ = x_prompt.shape
    bs, dec, _ = x_sample.shape
    n_p, n_s = bp * seq, bs * dec
    tm_p, tm_s = 512, n_s
    depth = norm_mix.shape[0]
    xp = x_prompt.reshape(n_p, D_MODEL)
    xs = x_sample.reshape(n_s, D_MODEL)
    c_init = jnp.zeros((bp, NH_A, DH_A, DH_A), F32)
    n_init = jnp.zeros((bp, NH_A, DH_A), F32)
    m_init = jnp.full((bp, NH_A), NEG, F32)
    n_kblocks = seq // MOBA_BLOCK
    moba_k_pages = cache_moba_k.transpose(0, 1, 3, 4, 2)
    moba_v_pages = cache_moba_v.transpose(0, 1, 3, 4, 2)
    diff_k_pages = cache_diff_k.reshape(cache_diff_k.shape[:2] + (PAGE * NH_C, 2 * DH_C))
    diff_v_pages = cache_diff_v.reshape(cache_diff_v.shape[:2] + (PAGE * NH_C, 2 * DH_C))

    outs = {k: [] for k in ("p_mk", "p_mv", "p_dk", "p_dv", "p_c", "p_n", "p_m",
                            "s_mk", "s_mv", "s_dk", "s_dv", "s_c", "s_n", "s_m")}
    for layer in range(depth):
        gn_mix = _row(norm_mix[layer])
        gn_ffn = _row(norm_ffn[layer])
        wg, wu, wd = _ffn_weights(w_gate[layer], w_up[layer], w_down[layer])
        if layer % 2 == 0:
            e = layer // 2
            weights, auxes, kinds = _even_weights(w_in_even[e], b_gate_even[e], g_q_moba[e], g_k_moba[e])
            g_out = _row(g_out_mlstm[e])
            wo = w_out_even[e].astype(BF16)

            forms_p = [["f32"], ["f32"], ["q16"], ["f32", "b16", "mean"], ["f32", "vt"]]
            a, gates, q16, kb, k16, kmean, vb, vt = norm_proj(xp, gn_mix, weights, auxes, kinds, forms_p, tm_p)
            ha, c1, n1, m1 = _mlstm_group(a, gates, c_init, n_init, m_init, g_out, bp, seq)
            hb = flash_prompt(q16, k16, vt, [kmean.reshape(bp, n_kblocks, W_B)], "moba", bp, seq)
            xp = mix_ffn(xp, [ha, hb], wo, gn_ffn, wg, wu, wd, tm_p)
            outs["p_mk"].append(kb.reshape(bp, seq, NH_B, DH_B))
            outs["p_mv"].append(vb.reshape(bp, seq, NH_B, DH_B))
            outs["p_c"].append(c1); outs["p_n"].append(n1); outs["p_m"].append(m1)

            a, gates, qb, kb, vb = norm_proj(xs, gn_mix, weights, auxes, kinds, [["f32"]] * 5, tm_s)
            ha, c1, n1, m1 = _mlstm_group(a, gates, state_mlstm_c[e], state_mlstm_n[e], state_mlstm_m[e],
                                          g_out, bs, dec)
            hb = moba_sample(page_table, qb, kb, vb, moba_k_pages, moba_v_pages, e)
            xs = mix_ffn(xs, [ha, hb], wo, gn_ffn, wg, wu, wd, tm_s)
            outs["s_mk"].append(kb.reshape(bs, dec, NH_B, DH_B))
            outs["s_mv"].append(vb.reshape(bs, dec, NH_B, DH_B))
            outs["s_c"].append(c1); outs["s_n"].append(n1); outs["s_m"].append(m1)
        else:
            o = layer // 2
            lambda_init = 0.8 - 0.6 * math.exp(-0.3 * layer)
            w_in = w_in_odd[o].astype(BF16)
            weights = [w_in[:, i * D_MODEL:(i + 1) * D_MODEL] for i in range(3)]
            auxes = [_row(jnp.tile(g_q_diff[o].reshape(-1), NH_C)), _row(jnp.tile(g_k_diff[o].reshape(-1), NH_C)),
                     jnp.zeros((1, D_MODEL), F32)]
            kinds = ["half", "half", "none"]
            lam = lam_diff[o].astype(F32)
            gsub = _row(g_sub_diff[o])
            wo = w_out_odd[o].astype(BF16)

            forms_p = [["q16"], ["f32", "b16"], ["f32", "vt"]]
            q16, k, k16, v, vt = norm_proj(xp, gn_mix, weights, auxes, kinds, forms_p, tm_p)
            att = flash_prompt(q16, k16, vt, [lam, gsub], "diff", bp, seq, lambda_init)
            xp = mix_ffn(xp, [att], wo, gn_ffn, wg, wu, wd, tm_p)
            outs["p_dk"].append(k.reshape(bp, seq, NH_C, 2 * DH_C))
            outs["p_dv"].append(v.reshape(bp, seq, NH_C, 2 * DH_C))

            q, k, v = norm_proj(xs, gn_mix, weights, auxes, kinds, [["f32"]] * 3, tm_s)
            att = diff_sample(page_table, q, k, v, lam, gsub, diff_k_pages, diff_v_pages, o, lambda_init)
            xs = mix_ffn(xs, [att], wo, gn_ffn, wg, wu, wd, tm_s)
            outs["s_dk"].append(k.reshape(bs, dec, NH_C, 2 * DH_C))
            outs["s_dv"].append(v.reshape(bs, dec, NH_C, 2 * DH_C))

    st = {k: jnp.stack(v) for k, v in outs.items()}
    return (xp.reshape(bp, seq, D_MODEL), xs.reshape(bs, dec, D_MODEL),
            st["p_mk"], st["p_mv"], st["p_dk"], st["p_dv"], st["p_c"], st["p_n"], st["p_m"],
            st["s_mk"], st["s_mv"], st["s_dk"], st["s_dv"], st["s_c"], st["s_n"], st["s_m"])
```

```python
import functools
import math

import jax
import jax.numpy as jnp
from jax import lax
from jax.experimental import pallas as pl
from jax.experimental.pallas import tpu as pltpu

F32 = jnp.float32
BF16 = jnp.bfloat16

D_MODEL = 1024
NH_A, DH_A = 4, 128
W_A = NH_A * DH_A
NH_B, DH_B = 8, 64
W_B = NH_B * DH_B
NH_C, DH_C = 8, 64
MLSTM_CHUNK = 128
MOBA_BLOCK = 256
MOBA_TOPK = 3
PAGE = 128
EPS = 1e-6
NEG = -1e30
LANES = 128
FF_CHUNK = 256
FLASH_Q_TILE = 512
LOG2E = 1.4426950408889634
VMEM_LIMIT = 56 * 1024 * 1024


def _dot(a, b):
    return jnp.dot(a, b, preferred_element_type=F32)


def _dot_nt(a, b):
    return lax.dot_general(a, b, (((1,), (1,)), ((), ())), preferred_element_type=F32)


def _rms(x, g):
    ms = jnp.mean(x * x, axis=-1, keepdims=True)
    return x * lax.rsqrt(ms + EPS) * g


def _half_norm(y, g):
    outs = []
    for c in range(y.shape[1] // LANES):
        yc = y[:, c * LANES:(c + 1) * LANES]
        sq = yc * yc
        lo = lax.broadcasted_iota(jnp.int32, yc.shape, 1) < 64
        s_lo = jnp.sum(jnp.where(lo, sq, 0.0), axis=-1, keepdims=True)
        s_hi = jnp.sum(jnp.where(lo, 0.0, sq), axis=-1, keepdims=True)
        ms = jnp.where(lo, s_lo, s_hi) * (1.0 / 64)
        outs.append(yc * lax.rsqrt(ms + EPS))
    return jnp.concatenate(outs, axis=-1) * g


def _params(sem):
    return pltpu.CompilerParams(dimension_semantics=sem, vmem_limit_bytes=VMEM_LIMIT)


def _resident(arr):
    nd = arr.ndim
    return pl.BlockSpec(arr.shape, lambda i: (0,) * nd, pipeline_mode=pl.Buffered(1))


def _form_shape(form, n_rows, width, tm):
    nb, tb = n_rows // MOBA_BLOCK, max(tm // MOBA_BLOCK, 1)
    if form == "f32":
        return (n_rows, width), F32, (tm, width), lambda i: (i, 0)
    if form in ("b16", "q16"):
        return (n_rows, width), BF16, (tm, width), lambda i: (i, 0)
    if form == "vt":
        return (nb, width // LANES, LANES, MOBA_BLOCK), BF16, (tb, width // LANES, LANES, MOBA_BLOCK), lambda i: (i, 0, 0, 0)
    assert form == "mean"
    return (nb, 1, width), F32, (tb, 1, width), lambda i: (i, 0, 0)


def _norm_proj_body(x_ref, gn_ref, *refs, kinds, forms):
    n = len(kinds)
    w_refs, aux_refs, out_refs = refs[:n], refs[n:2 * n], list(refs[2 * n:])
    h = _rms(x_ref[...], gn_ref[...]).astype(BF16)
    for idx, kind in enumerate(kinds):
        w_ref, aux_ref = w_refs[idx], aux_refs[idx]
        outs = {form: out_refs.pop(0) for form in forms[idx]}
        width = w_ref.shape[1]
        step = min(width, 512)
        for c0 in range(0, width, step):
            y = _dot(h, w_ref[:, c0:c0 + step])
            if kind == "bias":
                y = y + aux_ref[:, c0:c0 + step]
            elif kind == "half":
                y = _half_norm(y, aux_ref[:, c0:c0 + step])
            if "f32" in outs:
                outs["f32"][:, c0:c0 + step] = y
            if "b16" in outs:
                outs["b16"][:, c0:c0 + step] = y.astype(BF16)
            if "q16" in outs:
                outs["q16"][:, c0:c0 + step] = (y * (DH_B ** -0.5 * LOG2E)).astype(BF16)
            for r in range(y.shape[0] // MOBA_BLOCK):
                blk = y[r * MOBA_BLOCK:(r + 1) * MOBA_BLOCK, :]
                if "mean" in outs:
                    outs["mean"][r, :, c0:c0 + step] = jnp.mean(blk, axis=0, keepdims=True)
                if "vt" in outs:
                    for g in range(step // LANES):
                        tile = jnp.transpose(blk[:, g * LANES:(g + 1) * LANES])
                        outs["vt"][r, c0 // LANES + g] = tile.astype(BF16)


def norm_proj(x, gn, weights, auxes, kinds, forms, tm):
    n_rows = x.shape[0]
    in_specs = [pl.BlockSpec((tm, D_MODEL), lambda i: (i, 0)), _resident(gn)]
    in_specs += [_resident(w) for w in weights] + [_resident(a) for a in auxes]
    out_shape, out_specs = [], []
    for w, fs in zip(weights, forms):
        for form in fs:
            shape, dtype, block, imap = _form_shape(form, n_rows, w.shape[1], tm)
            out_shape.append(jax.ShapeDtypeStruct(shape, dtype))
            out_specs.append(pl.BlockSpec(block, imap))
    return pl.pallas_call(
        functools.partial(_norm_proj_body, kinds=tuple(kinds), forms=tuple(tuple(f) for f in forms)),
        grid=(n_rows // tm,), in_specs=in_specs, out_specs=out_specs, out_shape=out_shape,
        compiler_params=_params(("parallel",)), name="norm_proj",
    )(x, gn, *weights, *auxes)


def _mix_ffn_body(x_ref, *refs, n_mix):
    a_refs = refs[:n_mix]
    wo_ref, gn_ref, wg_ref, wu_ref, wd_ref, out_ref = refs[n_mix:]
    mixed = jnp.concatenate([a_ref[...] for a_ref in a_refs], axis=-1).astype(BF16)
    x1 = x_ref[...] + _dot(mixed, wo_ref[...])
    h = _rms(x1, gn_ref[...]).astype(BF16)
    acc = x1
    for c in range(wg_ref.shape[0]):
        g = _dot(h, wg_ref[c])
        u = _dot(h, wu_ref[c])
        act = (g * jax.nn.sigmoid(g) * u).astype(BF16)
        acc = acc + _dot(act, wd_ref[c])
    out_ref[...] = acc


def mix_ffn(x, mixes, wo, gn, wg, wu, wd, tm):
    n_rows = x.shape[0]
    in_specs = [pl.BlockSpec((tm, D_MODEL), lambda i: (i, 0))]
    in_specs += [pl.BlockSpec((tm, a.shape[1]), lambda i: (i, 0)) for a in mixes]
    in_specs += [_resident(w) for w in (wo, gn, wg, wu, wd)]
    return pl.pallas_call(
        functools.partial(_mix_ffn_body, n_mix=len(mixes)),
        grid=(n_rows // tm,), in_specs=in_specs,
        out_specs=pl.BlockSpec((tm, D_MODEL), lambda i: (i, 0)),
        out_shape=jax.ShapeDtypeStruct((n_rows, D_MODEL), F32),
        compiler_params=_params(("parallel",)), name="mix_ffn",
    )(x, *mixes, wo, gn, wg, wu, wd)


def _log_sigmoid(x):
    return jnp.minimum(x, 0.0) - jnp.log1p(jnp.exp(-jnp.abs(x)))


def _pad_rows(x, rows):
    if x.shape[0] == rows:
        return x
    return jnp.concatenate([x, jnp.zeros((rows - x.shape[0],) + x.shape[1:], x.dtype)], axis=0)


def _gate_prep_body(g_ref, col_ref, row_ref, *, valid_len):
    chunk = MLSTM_CHUNK
    for c in range(col_ref.shape[0] // chunk):
        g = _pad_rows(g_ref[c * valid_len:(c + 1) * valid_len, :], chunk)
        lane = lax.broadcasted_iota(jnp.int32, g.shape, 1)
        row = lax.broadcasted_iota(jnp.int32, g.shape, 0)
        is_f = (lane >= NH_A) & (lane < 2 * NH_A)
        live = row < valid_len
        logf = jnp.where(is_f & live, _log_sigmoid(g), 0.0)
        k = 1
        while k < chunk:
            logf = logf + jnp.where(row >= k, pltpu.roll(logf, k, 0), 0.0)
            k *= 2
        out = jnp.where(is_f, logf, jnp.where(live, g, NEG))
        col_ref[c * chunk:(c + 1) * chunk, :] = out
        row_ref[:, c * chunk:(c + 1) * chunk] = jnp.transpose(out)[:8, :]


def gate_prep(gates, valid_len):
    chunk = MLSTM_CHUNK
    n_chunks = gates.shape[0] // valid_len
    per_tile = 4 if (valid_len == chunk and n_chunks % 4 == 0) else 1
    return pl.pallas_call(
        functools.partial(_gate_prep_body, valid_len=valid_len),
        grid=(n_chunks // per_tile,),
        in_specs=[pl.BlockSpec((per_tile * valid_len, LANES), lambda i: (i, 0))],
        out_specs=[pl.BlockSpec((per_tile * chunk, LANES), lambda i: (i, 0)),
                   pl.BlockSpec((8, per_tile * chunk), lambda i: (0, i))],
        out_shape=[jax.ShapeDtypeStruct((n_chunks * chunk, LANES), F32),
                   jax.ShapeDtypeStruct((8, n_chunks * chunk), F32)],
        compiler_params=_params(("parallel",)), name="gate_prep",
    )(gates)


def _mlstm_body(q_ref, k_ref, v_ref, o_ref, gc_ref, gr_ref, c0_ref, n0_ref, m0_ref, gout_ref,
                h_ref, c_out, n_out, m_out, c_s, n_s, m_s):
    chunk_id = pl.program_id(1)
    L = MLSTM_CHUNK
    live = q_ref.shape[0]

    @pl.when(chunk_id == 0)
    def _():
        c_s[...] = c0_ref[...]
        n_s[...] = n0_ref[...]
        m_s[...] = m0_ref[...]

    gc = gc_ref[...]
    gr = gr_ref[...]
    lane = lax.broadcasted_iota(jnp.int32, gc.shape, 1)
    sub = lax.broadcasted_iota(jnp.int32, gr.shape, 0)
    tt = lax.broadcasted_iota(jnp.int32, (L, L), 0)
    ss = lax.broadcasted_iota(jnp.int32, (L, L), 1)
    for head in range(NH_A):
        cols = slice(head * DH_A, (head + 1) * DH_A)
        q = _pad_rows(q_ref[:, cols], L)
        k = _pad_rows(k_ref[:, cols], L) * (DH_A ** -0.5)
        v = _pad_rows(v_ref[:, cols], L)
        i_col = jnp.sum(jnp.where(lane == head, gc, 0.0), axis=-1, keepdims=True)
        b_col = jnp.sum(jnp.where(lane == head + NH_A, gc, 0.0), axis=-1, keepdims=True)
        i_row = jnp.sum(jnp.where(sub == head, gr, 0.0), axis=0, keepdims=True)
        b_row = jnp.sum(jnp.where(sub == head + NH_A, gr, 0.0), axis=0, keepdims=True)
        m_prev = m_s[head][:, 0:1]
        b_last = b_col[L - 1:L, :]
        c_prev = c_s[head]
        n_prev = n_s[head]

        log_d = jnp.where(ss <= tt, b_col - b_row + i_row, NEG)
        inter = b_col + m_prev
        m_t = jnp.maximum(inter, jnp.max(log_d, axis=-1, keepdims=True))
        w_intra = jnp.exp(log_d - m_t)
        w_inter = jnp.exp(inter - m_t)
        qb, kb, vb = q.astype(BF16), k.astype(BF16), v.astype(BF16)
        s = _dot_nt(qb, kb) * w_intra
        num = w_inter * _dot_nt(qb, c_prev.astype(BF16)) + _dot(s.astype(BF16), vb)
        den = w_inter * jnp.sum(q * n_prev, axis=-1, keepdims=True) + jnp.sum(s, axis=-1, keepdims=True)
        hh = num / jnp.maximum(jnp.abs(den), jnp.exp(-m_t))
        m_new = m_t[L - 1:L, :]
        w_c = jnp.exp(b_last + m_prev - m_new)
        w_s = jnp.exp(b_last - b_col + i_col - m_new)
        vw_t = jnp.transpose(v * w_s).astype(BF16)
        c_new = w_c * c_prev + _dot(vw_t, kb)
        n_new = w_c * n_prev + jnp.sum(w_s * k, axis=0, keepdims=True)
        m_row = jnp.broadcast_to(m_new, (1, LANES))
        c_s[head] = c_new
        n_s[head] = n_new
        m_s[head] = m_row

        hn = hh * lax.rsqrt(jnp.mean(hh * hh, axis=-1, keepdims=True) + EPS) * gout_ref[...]
        h_ref[:, cols] = hn[:live] * jax.nn.sigmoid(o_ref[:, cols])
        c_out[head] = c_new
        n_out[head] = n_new
        m_out[head] = m_row


def mlstm(a, g_col, g_row, c0, n0, m0, g_out, batch, n_chunks, live):
    L = MLSTM_CHUNK
    n_rows = a.shape[0]

    def part(p):
        return pl.BlockSpec((live, W_A), lambda b, c: (b * n_chunks + c, p))

    def state(shape):
        return pl.BlockSpec((NH_A,) + shape, lambda b, c: (b, 0, 0))

    in_specs = [part(0), part(1), part(2), part(3),
                pl.BlockSpec((L, LANES), lambda b, c: (b * n_chunks + c, 0)),
                pl.BlockSpec((8, L), lambda b, c: (0, b * n_chunks + c)),
                state((DH_A, DH_A)), state((1, DH_A)), state((1, LANES)),
                pl.BlockSpec((1, DH_A), lambda b, c: (0, 0))]
    out_specs = [part(0), state((DH_A, DH_A)), state((1, DH_A)), state((1, LANES))]
    out_shape = [jax.ShapeDtypeStruct((n_rows, W_A), F32),
                 jax.ShapeDtypeStruct((batch * NH_A, DH_A, DH_A), F32),
                 jax.ShapeDtypeStruct((batch * NH_A, 1, DH_A), F32),
                 jax.ShapeDtypeStruct((batch * NH_A, 1, LANES), F32)]
    return pl.pallas_call(
        _mlstm_body, grid=(batch, n_chunks), in_specs=in_specs, out_specs=out_specs,
        out_shape=out_shape,
        scratch_shapes=[pltpu.VMEM((NH_A, DH_A, DH_A), F32), pltpu.VMEM((NH_A, 1, DH_A), F32),
                        pltpu.VMEM((NH_A, 1, LANES), F32)],
        compiler_params=_params(("parallel", "arbitrary")), name="mlstm",
    )(a, a, a, a, g_col, g_row, c0, n0, m0, g_out)


def _topk_bias(gate, valid):
    lane = lax.broadcasted_iota(jnp.int32, gate.shape, 1)
    g = jnp.where(valid, gate, NEG)
    chosen = jnp.zeros(gate.shape, jnp.bool_)
    for _ in range(MOBA_TOPK):
        top = jnp.max(g, axis=-1, keepdims=True)
        first = jnp.min(jnp.where(g == top, lane, LANES), axis=-1, keepdims=True)
        pick = lane == first
        chosen = chosen | pick
        g = jnp.where(pick, NEG, g)
    return jnp.where(chosen & valid, 0.0, NEG)


def _lambda(lam_ref, lambda_init):
    lp = lam_ref[...]
    s01 = jnp.sum(lp[0:1, :] * lp[1:2, :], axis=-1, keepdims=True)
    s23 = jnp.sum(lp[2:3, :] * lp[3:4, :], axis=-1, keepdims=True)
    return jnp.exp(s01) - jnp.exp(s23) + lambda_init


def _topk_bias_rows(gate, valid):
    row = lax.broadcasted_iota(jnp.int32, gate.shape, 0)
    g = jnp.where(valid, gate, NEG)
    chosen = jnp.zeros(gate.shape, jnp.bool_)
    for _ in range(MOBA_TOPK):
        top = jnp.max(g, axis=0, keepdims=True)
        first = jnp.min(jnp.where(g == top, row, gate.shape[0]), axis=0, keepdims=True)
        pick = row == first
        chosen = chosen | pick
        g = jnp.where(pick, NEG, g)
    return jnp.where(chosen & valid, 0.0, NEG)


def _flash_body(*refs, mode, tq, tk, lambda_init):
    if mode == "moba":
        q_ref, k_ref, vt_ref, km_ref, o_ref, m_s, l_s, acc_s, s_buf, sel_s = refs
    else:
        q_ref, k_ref, vt_ref, lam_ref, gsub_ref, o_ref, m_s, l_s, acc_s, s_buf = refs
    per = tq // tk
    n_past = pl.program_id(2) * per
    lo = lax.broadcasted_iota(jnp.int32, (tq, LANES), 1) < 64
    q = q_ref[...]
    zero = jnp.zeros_like(q)
    q2 = jnp.concatenate([jnp.where(lo, q, zero), jnp.where(lo, zero, q)], axis=0)
    col = lax.broadcasted_iota(jnp.int32, (1, 2 * tq), 1)
    t_in_tile = jnp.where(col >= tq, col - tq, col)

    def scores(j):
        start = pl.multiple_of(j * tk, tk)
        return _dot_nt(k_ref[pl.ds(start, tk), :], q2)

    def absorb(j, sj):
        m_old = m_s[...]
        m_new = jnp.maximum(m_old, jnp.max(sj, axis=0, keepdims=True))
        alpha = jnp.exp2(m_old - m_new)
        pj = jnp.exp2(sj - m_new)
        l_s[...] = alpha * l_s[...] + jnp.sum(pj, axis=0, keepdims=True)
        acc_s[...] = alpha * acc_s[...] + _dot(vt_ref[j, 0], pj.astype(BF16))
        m_s[...] = m_new

    m_s[...] = jnp.full(m_s.shape, NEG, F32)
    l_s[...] = jnp.zeros(l_s.shape, F32)
    acc_s[...] = jnp.zeros(acc_s.shape, F32)
    if mode == "moba":
        gate = _dot_nt(km_ref[0].astype(BF16), q2)
        row = lax.broadcasted_iota(jnp.int32, gate.shape, 0)
        own = n_past + t_in_tile // tk
        sel_s[...] = jnp.where(row == own, 0.0, _topk_bias_rows(gate, row < own))

    s_buf[...] = scores(0)

    def body(j, carry):
        sj = s_buf[...]
        s_buf[...] = scores(j + 1)
        if mode == "moba":
            sj = sj + sel_s[pl.ds(j, 1), :]
        absorb(j, sj)
        return carry

    lax.fori_loop(0, n_past, body, 0)

    key = lax.broadcasted_iota(jnp.int32, (tk, 2 * tq), 0)
    for d in range(per):
        sd = s_buf[...]
        if d + 1 < per:
            s_buf[...] = scores(n_past + d + 1)
        if mode == "moba":
            sd = sd + sel_s[pl.ds(n_past + d, 1), :]
        absorb(n_past + d, jnp.where(key + d * tk <= t_in_tile, sd, NEG))

    o2t = acc_s[...] / l_s[...]
    c1 = jnp.transpose(o2t[:, :tq])
    c2 = jnp.transpose(o2t[:, tq:])
    if mode == "moba":
        o_ref[...] = jnp.where(lo, c1, c2)
    else:
        o = c1 - _lambda(lam_ref, lambda_init) * c2
        on = o * lax.rsqrt(jnp.mean(o * o, axis=-1, keepdims=True) + EPS) * gsub_ref[...]
        o_ref[...] = on * (1.0 - lambda_init)


def flash_prompt(q, k, vt, extras, mode, batch, seq, lambda_init=0.0):
    tk = MOBA_BLOCK
    tq = FLASH_Q_TILE
    groups = q.shape[1] // LANES
    nq, nk = seq // tq, seq // tk
    in_specs = [pl.BlockSpec((tq, LANES), lambda b, g, i: (b * nq + i, g)),
                pl.BlockSpec((seq, LANES), lambda b, g, i: (b, g)),
                pl.BlockSpec((nk, 1, LANES, tk), lambda b, g, i: (b, g, 0, 0))]
    scratch = [pltpu.VMEM((1, 2 * tq), F32), pltpu.VMEM((1, 2 * tq), F32), pltpu.VMEM((LANES, 2 * tq), F32),
               pltpu.VMEM((tk, 2 * tq), F32)]
    if mode == "moba":
        in_specs.append(pl.BlockSpec((1, nk, LANES), lambda b, g, i: (b, 0, g)))
        scratch.append(pltpu.VMEM((nk, 2 * tq), F32))
    else:
        in_specs += [pl.BlockSpec((4, DH_C), lambda b, g, i: (0, 0)), pl.BlockSpec((1, LANES), lambda b, g, i: (0, 0))]
    return pl.pallas_call(
        functools.partial(_flash_body, mode=mode, tq=tq, tk=tk, lambda_init=lambda_init),
        grid=(batch, groups, nq), in_specs=in_specs,
        out_specs=pl.BlockSpec((tq, LANES), lambda b, g, i: (b * nq + i, g)),
        out_shape=jax.ShapeDtypeStruct(q.shape, F32), scratch_shapes=scratch,
        compiler_params=_params(("parallel", "parallel", "arbitrary")), name="flash_" + mode,
    )(q, k, vt, *extras)


PAGES_PER_STEP = 8


def _page_specs(layer, page_shape):
    def spec(r):
        return pl.BlockSpec((1, 1) + page_shape,
                            lambda b, s, pt: (layer, pt[b, PAGES_PER_STEP * s + r]) + (0,) * len(page_shape))
    return [spec(r) for r in range(PAGES_PER_STEP)]


def _moba_sample_body(pt_ref, q_ref, kn_ref, vn_ref, *refs, t_new):
    k_refs, v_refs = refs[:PAGES_PER_STEP], refs[PAGES_PER_STEP:2 * PAGES_PER_STEP]
    o_ref, part_s, m_s, l_s, g_s = refs[2 * PAGES_PER_STEP:]
    step = pl.program_id(1)
    n_steps = pl.num_programs(1)
    rows = NH_B * t_new
    pages_per_block = MOBA_BLOCK // PAGE
    blocks_per_step = PAGES_PER_STEP // pages_per_block
    lane = lax.broadcasted_iota(jnp.int32, (rows, LANES), 1)

    @pl.when(step == 0)
    def _():
        m_s[...] = jnp.full(m_s.shape, NEG, F32)
        l_s[...] = jnp.zeros(l_s.shape, F32)
        g_s[...] = jnp.full(g_s.shape, NEG, F32)

    qs = q_ref[...] * (DH_B ** -0.5)
    q_heads = [qs[:, h * DH_B:(h + 1) * DH_B].astype(BF16) for h in range(NH_B)]

    def block_partial(score, weighted_values, mask):
        s = jnp.concatenate([score(h) for h in range(NH_B)], axis=0)
        gate = jnp.sum(s, axis=-1, keepdims=True)
        if mask is not None:
            s = jnp.where(mask, s, NEG)
        m = jnp.max(s, axis=-1, keepdims=True)
        p = jnp.exp(s - m)
        l = jnp.sum(p, axis=-1, keepdims=True)
        o = jnp.concatenate([weighted_values(h, p[h * t_new:(h + 1) * t_new].astype(BF16))
                             for h in range(NH_B)], axis=0)
        return m, l, gate, o

    for r in range(blocks_per_step):
        pages = range(r * pages_per_block, (r + 1) * pages_per_block)

        def score(h, pages=pages):
            kt = jnp.concatenate([k_refs[p][0, 0, h] for p in pages], axis=1).astype(BF16)
            return _dot(q_heads[h], kt)

        def weighted_values(h, ph, pages=pages):
            vt = jnp.concatenate([v_refs[p][0, 0, h] for p in pages], axis=1).astype(BF16)
            return _dot_nt(ph, vt)

        m, l, gate, o = block_partial(score, weighted_values, None)
        j = step * blocks_per_step + r
        part_s[j] = o
        here = lane == j
        m_s[...] = jnp.where(here, m, m_s[...])
        l_s[...] = jnp.where(here, l, l_s[...])
        g_s[...] = jnp.where(here, gate, g_s[...])

    @pl.when(step == n_steps - 1)
    def _():
        n_past = n_steps * blocks_per_step
        pad = jnp.zeros((LANES - t_new, W_B), F32)
        kn = jnp.concatenate([kn_ref[...], pad], axis=0)
        vn = jnp.concatenate([vn_ref[...], pad], axis=0)
        rr = lax.broadcasted_iota(jnp.int32, (rows, LANES), 0)
        causal = lane <= rr % t_new
        m_own, l_own, _, o_own = block_partial(
            lambda h: _dot_nt(q_heads[h], kn[:, h * DH_B:(h + 1) * DH_B].astype(BF16)),
            lambda h, ph: _dot(ph, vn[:, h * DH_B:(h + 1) * DH_B].astype(BF16)), causal)
        bias = _topk_bias(g_s[...], lane < n_past)
        m_all = jnp.maximum(jnp.max(m_s[...] + bias, axis=-1, keepdims=True), m_own)
        w = jnp.exp(m_s[...] + bias - m_all)
        w_own = jnp.exp(m_own - m_all)
        den = jnp.sum(w * l_s[...], axis=-1, keepdims=True) + w_own * l_own
        acc = w_own * o_own
        for j in range(n_past):
            acc = acc + w[:, j:j + 1] * part_s[j]
        out = acc / den
        o_ref[...] = jnp.concatenate([out[h * t_new:(h + 1) * t_new] for h in range(NH_B)], axis=-1)


def moba_sample(page_table, qn, kn, vn, cache_k, cache_v, layer):
    batch, n_pages = page_table.shape
    t_new = qn.shape[0] // batch
    n_steps = n_pages // PAGES_PER_STEP
    n_blocks = n_pages * PAGE // MOBA_BLOCK
    rows = NH_B * t_new
    new = pl.BlockSpec((t_new, W_B), lambda b, s, pt: (b, 0))
    grid_spec = pltpu.PrefetchScalarGridSpec(
        num_scalar_prefetch=1, grid=(batch, n_steps),
        in_specs=[new, new, new] + 2 * _page_specs(layer, (NH_B, DH_B, PAGE)),
        out_specs=new,
        scratch_shapes=[pltpu.VMEM((n_blocks, rows, DH_B), F32), pltpu.VMEM((rows, LANES), F32),
                        pltpu.VMEM((rows, LANES), F32), pltpu.VMEM((rows, LANES), F32)])
    return pl.pallas_call(
        functools.partial(_moba_sample_body, t_new=t_new), grid_spec=grid_spec,
        out_shape=jax.ShapeDtypeStruct(qn.shape, F32),
        compiler_params=_params(("parallel", "arbitrary")), name="moba_sample",
    )(page_table, qn, kn, vn, *([cache_k] * PAGES_PER_STEP), *([cache_v] * PAGES_PER_STEP))


def _diff_sample_body(pt_ref, q_ref, kn_ref, vn_ref, lam_ref, gsub_ref, *refs, t_new, lambda_init):
    k_refs, v_refs = refs[:PAGES_PER_STEP], refs[PAGES_PER_STEP:2 * PAGES_PER_STEP]
    o_ref, m_s, l_s, acc_s = refs[2 * PAGES_PER_STEP:]
    step = pl.program_id(1)
    n_steps = pl.num_programs(1)
    per_head = 2 * t_new
    rows = NH_C * per_head

    @pl.when(step == 0)
    def _():
        m_s[...] = jnp.full(m_s.shape, NEG, F32)
        l_s[...] = jnp.zeros(l_s.shape, F32)
        acc_s[...] = jnp.zeros(acc_s.shape, F32)

    qs = q_ref[...] * (DH_C ** -0.5)
    lo = lax.broadcasted_iota(jnp.int32, (t_new, LANES), 1) < DH_C
    q_heads = []
    for h in range(NH_C):
        qh = qs[:, h * LANES:(h + 1) * LANES]
        q_heads.append(jnp.concatenate([jnp.where(lo, qh, 0.0), jnp.where(lo, 0.0, qh)], axis=0).astype(BF16))

    def update(k_of_head, v_of_head, mask):
        s = jnp.concatenate([_dot_nt(q_heads[h], k_of_head(h)) for h in range(NH_C)], axis=0)
        if mask is not None:
            s = jnp.where(mask, s, NEG)
        m_old = m_s[...]
        m_new = jnp.maximum(m_old, jnp.max(s, axis=-1, keepdims=True))
        alpha = jnp.exp(m_old - m_new)
        p = jnp.exp(s - m_new)
        l_s[...] = alpha * l_s[...] + jnp.sum(p, axis=-1, keepdims=True)
        pv = jnp.concatenate([_dot(p[h * per_head:(h + 1) * per_head].astype(BF16), v_of_head(h))
                              for h in range(NH_C)], axis=0)
        acc_s[...] = alpha * acc_s[...] + pv
        m_s[...] = m_new

    def head_rows(page_refs, h):
        rows_h = [r[0, 0, pl.ds(h, PAGE, stride=NH_C), :] for r in page_refs]
        return jnp.concatenate(rows_h, axis=0).astype(BF16)

    update(lambda h: head_rows(k_refs, h), lambda h: head_rows(v_refs, h), None)

    @pl.when(step == n_steps - 1)
    def _():
        pad = jnp.zeros((LANES - t_new, NH_C * LANES), F32)
        kn = jnp.concatenate([kn_ref[...], pad], axis=0)
        vn = jnp.concatenate([vn_ref[...], pad], axis=0)
        rr = lax.broadcasted_iota(jnp.int32, (rows, LANES), 0)
        cc = lax.broadcasted_iota(jnp.int32, (rows, LANES), 1)
        update(lambda h: kn[:, h * LANES:(h + 1) * LANES].astype(BF16),
               lambda h: vn[:, h * LANES:(h + 1) * LANES].astype(BF16), cc <= rr % t_new)
        o2 = acc_s[...] / l_s[...]
        lam = _lambda(lam_ref, lambda_init)
        outs = []
        for h in range(NH_C):
            base = h * per_head
            o = o2[base:base + t_new] - lam * o2[base + t_new:base + per_head]
            on = o * lax.rsqrt(jnp.mean(o * o, axis=-1, keepdims=True) + EPS) * gsub_ref[...]
            outs.append(on * (1.0 - lambda_init))
        o_ref[...] = jnp.concatenate(outs, axis=-1)


def diff_sample(page_table, qn, kn, vn, lam, gsub, cache_k, cache_v, layer, lambda_init):
    batch, n_pages = page_table.shape
    t_new = qn.shape[0] // batch
    n_steps = n_pages // PAGES_PER_STEP
    rows = NH_C * 2 * t_new
    new = pl.BlockSpec((t_new, NH_C * LANES), lambda b, s, pt: (b, 0))
    grid_spec = pltpu.PrefetchScalarGridSpec(
        num_scalar_prefetch=1, grid=(batch, n_steps),
        in_specs=[new, new, new,
                  pl.BlockSpec((4, DH_C), lambda b, s, pt: (0, 0)),
                  pl.BlockSpec((1, LANES), lambda b, s, pt: (0, 0))]
        + 2 * _page_specs(layer, (PAGE * NH_C, LANES)),
        out_specs=new,
        scratch_shapes=[pltpu.VMEM((rows, 1), F32), pltpu.VMEM((rows, 1), F32), pltpu.VMEM((rows, LANES), F32)])
    return pl.pallas_call(
        functools.partial(_diff_sample_body, t_new=t_new, lambda_init=lambda_init), grid_spec=grid_spec,
        out_shape=jax.ShapeDtypeStruct(qn.shape, F32),
        compiler_params=_params(("parallel", "arbitrary")), name="diff_sample",
    )(page_table, qn, kn, vn, lam, gsub, *([cache_k] * PAGES_PER_STEP), *([cache_v] * PAGES_PER_STEP))


def _row(v):
    return v.reshape(1, -1).astype(F32)


def _even_weights(w_in, b_gate, g_q, g_k):
    w_a = w_in[:, :4 * W_A].astype(BF16)
    g0 = 4 * W_A
    w_g = jnp.pad(w_in[:, g0:g0 + 2 * NH_A], ((0, 0), (0, LANES - 2 * NH_A))).astype(BF16)
    b0 = g0 + 2 * NH_A
    w_q, w_k, w_v = (w_in[:, b0 + i * W_B:b0 + (i + 1) * W_B].astype(BF16) for i in range(3))
    bias = jnp.pad(_row(b_gate), ((0, 0), (0, LANES - 2 * NH_A)))
    zeros = jnp.zeros((1, W_B), F32)
    weights = [w_a, w_g, w_q, w_k, w_v]
    auxes = [jnp.zeros((1, 4 * W_A), F32), bias, _row(jnp.tile(g_q, NH_B)), _row(jnp.tile(g_k, NH_B)), zeros]
    return weights, auxes, ["none", "bias", "half", "half", "none"]


def _ffn_weights(w_gate, w_up, w_down):
    d_ff = w_gate.shape[1]
    n = d_ff // FF_CHUNK
    wg = w_gate.astype(BF16).reshape(D_MODEL, n, FF_CHUNK).transpose(1, 0, 2)
    wu = w_up.astype(BF16).reshape(D_MODEL, n, FF_CHUNK).transpose(1, 0, 2)
    wd = w_down.astype(BF16).reshape(n, FF_CHUNK, D_MODEL)
    return wg, wu, wd


def _mlstm_group(a, gates, c0, n0, m0, g_out, batch, seq):
    L = MLSTM_CHUNK
    live, n_chunks = (L, seq // L) if seq % L == 0 else (seq, 1)
    g_col, g_row = gate_prep(gates, live)
    ha, c1, n1, m1 = mlstm(a, g_col, g_row, c0.reshape(batch * NH_A, DH_A, DH_A),
                           n0.reshape(batch * NH_A, 1, DH_A),
                           jnp.broadcast_to(m0.reshape(batch * NH_A, 1, 1), (batch * NH_A, 1, LANES)),
                           g_out, batch, n_chunks, live)
    return (ha, c1.reshape(batch, NH_A, DH_A, DH_A), n1.reshape(batch, NH_A, DH_A),
            m1[:, 0, 0].reshape(batch, NH_A))


def kernel(x_prompt, x_sample, cache_moba_k, cache_moba_v, cache_diff_k, cache_diff_v, state_mlstm_c, state_mlstm_n, state_mlstm_m, page_table, norm_mix, norm_ffn, w_in_even, b_gate_even, g_out_mlstm, g_q_moba, g_k_moba, w_out_even, w_in_odd, g_q_diff, g_k_diff, lam_diff, g_sub_diff, w_out_odd, w_gate, w_up, w_down):
    bp, seq, _ = x_prompt.shape
    bs, dec, _ = x_sample.shape
    n_p, n_s = bp * seq, bs * dec
    tm_p, tm_s = 512, n_s
    depth = norm_mix.shape[0]
    xp = x_prompt.reshape(n_p, D_MODEL)
    xs = x_sample.reshape(n_s, D_MODEL)
    c_init = jnp.zeros((bp, NH_A, DH_A, DH_A), F32)
    n_init = jnp.zeros((bp, NH_A, DH_A), F32)
    m_init = jnp.full((bp, NH_A), NEG, F32)
    n_kblocks = seq // MOBA_BLOCK
    moba_k_pages = cache_moba_k.transpose(0, 1, 3, 4, 2)
    moba_v_pages = cache_moba_v.transpose(0, 1, 3, 4, 2)
    diff_k_pages = cache_diff_k.reshape(cache_diff_k.shape[:2] + (PAGE * NH_C, 2 * DH_C))
    diff_v_pages = cache_diff_v.reshape(cache_diff_v.shape[:2] + (PAGE * NH_C, 2 * DH_C))

    outs = {k: [] for k in ("p_mk", "p_mv", "p_dk", "p_dv", "p_c", "p_n", "p_m",
                            "s_mk", "s_mv", "s_dk", "s_dv", "s_c", "s_n", "s_m")}
    for layer in range(depth):
        gn_mix = _row(norm_mix[layer])
        gn_ffn = _row(norm_ffn[layer])
        wg, wu, wd = _ffn_weights(w_gate[layer], w_up[layer], w_down[layer])
        if layer % 2 == 0:
            e = layer // 2
            weights, auxes, kinds = _even_weights(w_in_even[e], b_gate_even[e], g_q_moba[e], g_k_moba[e])
            g_out = _row(g_out_mlstm[e])
            wo = w_out_even[e].astype(BF16)

            forms_p = [["f32"], ["f32"], ["q16"], ["f32", "b16", "mean"], ["f32", "vt"]]
            a, gates, q16, kb, k16, kmean, vb, vt = norm_proj(xp, gn_mix, weights, auxes, kinds, forms_p, tm_p)
            ha, c1, n1, m1 = _mlstm_group(a, gates, c_init, n_init, m_init, g_out, bp, seq)
            hb = flash_prompt(q16, k16, vt, [kmean.reshape(bp, n_kblocks, W_B)], "moba", bp, seq)
            xp = mix_ffn(xp, [ha, hb], wo, gn_ffn, wg, wu, wd, tm_p)
            outs["p_mk"].append(kb.reshape(bp, seq, NH_B, DH_B))
            outs["p_mv"].append(vb.reshape(bp, seq, NH_B, DH_B))
            outs["p_c"].append(c1); outs["p_n"].append(n1); outs["p_m"].append(m1)

            a, gates, qb, kb, vb = norm_proj(xs, gn_mix, weights, auxes, kinds, [["f32"]] * 5, tm_s)
            ha, c1, n1, m1 = _mlstm_group(a, gates, state_mlstm_c[e], state_mlstm_n[e], state_mlstm_m[e],
                                          g_out, bs, dec)
            hb = moba_sample(page_table, qb, kb, vb, moba_k_pages, moba_v_pages, e)
            xs = mix_ffn(xs, [ha, hb], wo, gn_ffn, wg, wu, wd, tm_s)
            outs["s_mk"].append(kb.reshape(bs, dec, NH_B, DH_B))
            outs["s_mv"].append(vb.reshape(bs, dec, NH_B, DH_B))
            outs["s_c"].append(c1); outs["s_n"].append(n1); outs["s_m"].append(m1)
        else:
            o = layer // 2
            lambda_init = 0.8 - 0.6 * math.exp(-0.3 * layer)
            w_in = w_in_odd[o].astype(BF16)
            weights = [w_in[:, i * D_MODEL:(i + 1) * D_MODEL] for i in range(3)]
            auxes = [_row(jnp.tile(g_q_diff[o].reshape(-1), NH_C)), _row(jnp.tile(g_k_diff[o].reshape(-1), NH_C)),
                     jnp.zeros((1, D_MODEL), F32)]
            kinds = ["half", "half", "none"]
            lam = lam_diff[o].astype(F32)
            gsub = _row(g_sub_diff[o])
            wo = w_out_odd[o].astype(BF16)

            forms_p = [["q16"], ["f32", "b16"], ["f32", "vt"]]
            q16, k, k16, v, vt = norm_proj(xp, gn_mix, weights, auxes, kinds, forms_p, tm_p)
            att = flash_prompt(q16, k16, vt, [lam, gsub], "diff", bp, seq, lambda_init)
            xp = mix_ffn(xp, [att], wo, gn_ffn, wg, wu, wd, tm_p)
            outs["p_dk"].append(k.reshape(bp, seq, NH_C, 2 * DH_C))
            outs["p_dv"].append(v.reshape(bp, seq, NH_C, 2 * DH_C))

            q, k, v = norm_proj(xs, gn_mix, weights, auxes, kinds, [["f32"]] * 3, tm_s)
            att = diff_sample(page_table, q, k, v, lam, gsub, diff_k_pages, diff_v_pages, o, lambda_init)
            xs = mix_ffn(xs, [att], wo, gn_ffn, wg, wu, wd, tm_s)
            outs["s_dk"].append(k.reshape(bs, dec, NH_C, 2 * DH_C))
            outs["s_dv"].append(v.reshape(bs, dec, NH_C, 2 * DH_C))

    st = {k: jnp.stack(v) for k, v in outs.items()}
    return (xp.reshape(bp, seq, D_MODEL), xs.reshape(bs, dec, D_MODEL),
            st["p_mk"], st["p_mv"], st["p_dk"], st["p_dv"], st["p_c"], st["p_n"], st["p_m"],
            st["s_mk"], st["s_mv"], st["s_dk"], st["s_dv"], st["s_c"], st["s_n"], st["s_m"])
```
